```python
import jax, jax.numpy as jnp
from jax import lax
import numpy as np

D_MODEL = 2048
BATCH = 4
SEQ = 8192
DEPTH = 2

PLE_DIM = 256
N_BRANCH = 3
CONV_DIM = 1024
CONV_WIDTH = 31
RWKV_DIM = 1024
RWKV_HEAD = 64
RWKV_HEADS = RWKV_DIM // RWKV_HEAD
DECAY_LORA = 64
AAA_LORA = 64
MV_LORA = 32
GATE_LORA = 160
RWKV_LN_EPS = 64e-5
RWKV_BASE = 3 * RWKV_DIM + DECAY_LORA + AAA_LORA + GATE_LORA
HGRN_DIM = 1024
HGRN_EXPAND = 128
HGRN_HEADS = HGRN_DIM // HGRN_EXPAND
HGRN_DV = HGRN_DIM // HGRN_HEADS
HGRN_CHUNK = 64
D_FF = 7168
N_EXPERTS = 8
TOP_K = 2
LN_EPS = 1e-5
DN_ALPHA = (2 * DEPTH) ** 0.25
DN_BETA = (8 * DEPTH) ** -0.25
W_IN_BASE = 2 * CONV_DIM + RWKV_BASE + 4 * HGRN_DIM + N_BRANCH * D_MODEL

kernel_name = 'hybrid_conv_rwkv7_hgrn2_moe_deepnorm'


def _split(z, sizes):
    idx = [int(s) for s in np.cumsum(sizes)[:-1]]
    return jnp.split(z, idx, axis=-1)


def _heads(z, n):
    return z.reshape(z.shape[0], z.shape[1], n, -1)


def _token_shift(z):
    return jnp.pad(z[:, :-1], ((0, 0), (1, 0), (0, 0)))


def layer_norm(x, g, b, eps=LN_EPS):
    xf = x.astype(jnp.float32)
    mu = jnp.mean(xf, -1, keepdims=True)
    var = jnp.mean(jnp.square(xf - mu), -1, keepdims=True)
    return ((xf - mu) * lax.rsqrt(var + eps) * g + b).astype(x.dtype)


def conformer_conv(z, dw, dw_b, ln_g, ln_b):
    val, gate = jnp.split(z, 2, axis=-1)
    u = val * jax.nn.sigmoid(gate)
    u = lax.conv_general_dilated(
        u, dw.astype(jnp.float32)[:, None, :], (1,), [(CONV_WIDTH - 1, 0)],
        dimension_numbers=('NWC', 'WIO', 'NWC'),
        feature_group_count=CONV_DIM) + dw_b
    return jax.nn.silu(layer_norm(u, ln_g, ln_b))


def _rwkv7_scan(r, w, k, v, a, b):
    bsz, _, h, n = r.shape
    decay = jnp.exp(-jnp.exp(w))

    def step(state, inp):
        r_t, d_t, k_t, v_t, a_t, b_t = inp
        sa = jnp.einsum('bhvk,bhk->bhv', state, a_t)
        state = (state * d_t[:, :, None, :] + sa[..., None] * b_t[:, :, None, :]
                 + v_t[..., None] * k_t[:, :, None, :])
        return state, jnp.einsum('bhvk,bhk->bhv', state, r_t)

    xs = tuple(jnp.moveaxis(t, 1, 0) for t in (r, decay, k, v, a, b))
    _, y = lax.scan(step, jnp.zeros((bsz, h, n, n), jnp.float32), xs)
    return jnp.moveaxis(y, 0, 1)


def rwkv7_time_mix(z, v_first, w0, w2, a0, a2, g2, k_k, k_a, r_k, lnx_g, lnx_b, v0, v2):
    vres = v0 is not None
    parts = _split(z, [RWKV_DIM, RWKV_DIM, RWKV_DIM, DECAY_LORA, AAA_LORA, GATE_LORA]
                   + ([MV_LORA] if vres else []))
    zr, zk, zv, zw, za, zg = parts[:6]
    bsz, seq, _ = zr.shape
    w = -jax.nn.softplus(-(w0 + jnp.tanh(zw) @ w2)) - 0.5
    a = jax.nn.sigmoid(a0 + za @ a2)
    g = jax.nn.sigmoid(zg) @ g2
    v = zv
    if vres:
        v = v + (v_first - v) * jax.nn.sigmoid(v0 + parts[6] @ v2)
    else:
        v_first = v
    kk = _heads(zk * k_k, RWKV_HEADS)
    kk = kk / jnp.maximum(jnp.sqrt(jnp.sum(kk * kk, -1, keepdims=True)), 1e-12)
    k = zk * (1.0 + (a - 1.0) * k_a)
    rh, kh, vh = _heads(zr, RWKV_HEADS), _heads(k, RWKV_HEADS), _heads(v, RWKV_HEADS)
    ah = _heads(a, RWKV_HEADS)
    y = _rwkv7_scan(rh, _heads(w, RWKV_HEADS), kh, vh, -kk, kk * ah)
    mu = jnp.mean(y, -1, keepdims=True)
    var = jnp.mean(jnp.square(y - mu), -1, keepdims=True)
    y = ((y - mu) * lax.rsqrt(var + RWKV_LN_EPS)).reshape(bsz, seq, RWKV_DIM) * lnx_g + lnx_b
    y = y + (jnp.sum(rh * kh * r_k, -1, keepdims=True) * vh).reshape(bsz, seq, RWKV_DIM)
    return y * g, v_first


def _chunk_gla(q, k, v, log_f):
    bsz, seq, h, dk = q.shape
    dv = v.shape[-1]
    n = seq // HGRN_CHUNK

    def chunks(t):
        return t.reshape(bsz, n, HGRN_CHUNK, h, t.shape[-1]).transpose(1, 0, 3, 2, 4)

    causal = jnp.tril(jnp.ones((HGRN_CHUNK, HGRN_CHUNK), bool))

    def step(state, inp):
        qc, kc, vc, gc = inp
        b = jnp.cumsum(gc, axis=2)
        diff = b[:, :, :, None, :] - b[:, :, None, :, :]
        pair_decay = jnp.where(causal[:, :, None], jnp.exp(jnp.minimum(diff, 0.0)), 0.0)
        att = jnp.einsum('bhtd,bhsd,bhtsd->bhts', qc, kc, pair_decay)
        o = (jnp.einsum('bhts,bhse->bhte', att, vc)
             + jnp.einsum('bhtd,bhde->bhte', qc * jnp.exp(b), state))
        b_last = b[:, :, -1:, :]
        state = (jnp.exp(b_last[:, :, 0, :])[..., None] * state
                 + jnp.einsum('bhsd,bhse->bhde', kc * jnp.exp(b_last - b), vc))
        return state, o

    _, o = lax.scan(step, jnp.zeros((bsz, h, dk, dv), jnp.float32),
                    (chunks(q), chunks(k), chunks(v), chunks(log_f)))
    return o.transpose(1, 0, 3, 2, 4).reshape(bsz, seq, h, dv)


def hgrn2_mix(z, lb, norm_g):
    zq, zf, zi, zg = jnp.split(z, 4, axis=-1)
    bsz, seq, _ = zq.shape
    q = jax.nn.silu(zq) * HGRN_EXPAND ** -0.5
    k = (1.0 - lb) * jax.nn.sigmoid(-zf)
    log_f = jnp.log1p(-k)
    o = _chunk_gla(_heads(q, HGRN_HEADS), _heads(k, HGRN_HEADS),
                   _heads(zi, HGRN_HEADS), _heads(log_f, HGRN_HEADS))
    o = o * lax.rsqrt(jnp.mean(jnp.square(o), -1, keepdims=True) + LN_EPS) * norm_g
    return o.reshape(bsz, seq, HGRN_DIM) * jax.nn.silu(zg)


def swiglu(h, wg, wu, wd):
    return (jax.nn.silu(h @ wg) * (h @ wu)) @ wd


def moe_swiglu(h, router, wg, wu, wd):
    logits = (h @ router).astype(jnp.float32)
    top_val, top_idx = lax.top_k(logits, TOP_K)
    top_w = jax.nn.softmax(top_val, axis=-1)
    gate = jnp.sum(jax.nn.one_hot(top_idx, N_EXPERTS, dtype=jnp.float32) * top_w[..., None], axis=-2)
    gate = gate.astype(h.dtype)
    out = jnp.zeros_like(h)
    for e in range(N_EXPERTS):
        out = out + gate[..., e:e + 1] * swiglu(h, wg[e], wu[e], wd[e])
    return out


def setup_inputs(seed: int = 0) -> dict:
    key = jax.random.key(seed)
    ks = iter(jax.random.split(key, 48))
    f32 = jnp.float32

    def nrm(shape, scale=1.0):
        return jax.random.normal(next(ks), shape, f32) * scale

    def near_one(shape, s=0.02):
        return 1.0 + nrm(shape, s)

    L, Lr = DEPTH, DEPTH - 1
    n_dense, n_moe = (DEPTH + 1) // 2, DEPTH // 2
    ramp = (jnp.arange(RWKV_DIM, dtype=f32) / (RWKV_DIM - 1)) ** 0.85
    return {
        'x': nrm((BATCH, SEQ, D_MODEL)),
        'p': nrm((DEPTH, BATCH, SEQ, PLE_DIM)),
        'w_in': nrm((L, D_MODEL, W_IN_BASE), D_MODEL ** -0.5),
        'w_in_vres': nrm((Lr, D_MODEL, MV_LORA), D_MODEL ** -0.5),
        'conv_dw': nrm((L, CONV_WIDTH, CONV_DIM), CONV_WIDTH ** -0.5),
        'conv_dw_b': nrm((L, CONV_DIM), 0.01),
        'conv_ln_g': near_one((L, CONV_DIM)),
        'conv_ln_b': nrm((L, CONV_DIM), 0.02),
        'rwkv_mu': jax.random.uniform(next(ks), (L, RWKV_BASE), f32),
        'rwkv_mu_vres': jax.random.uniform(next(ks), (Lr, MV_LORA), f32),
        'rwkv_w0': -6.5 + 5.0 * ramp + nrm((L, RWKV_DIM), 0.1),
        'rwkv_w2': nrm((L, DECAY_LORA, RWKV_DIM), 0.1 * DECAY_LORA ** -0.5),
        'rwkv_a0': nrm((L, RWKV_DIM), 0.1),
        'rwkv_a2': nrm((L, AAA_LORA, RWKV_DIM), 0.1 * AAA_LORA ** -0.5),
        'rwkv_v0': near_one((Lr, RWKV_DIM), 0.1),
        'rwkv_v2': nrm((Lr, MV_LORA, RWKV_DIM), 0.1 * MV_LORA ** -0.5),
        'rwkv_g2': nrm((L, GATE_LORA, RWKV_DIM), GATE_LORA ** -0.5),
        'rwkv_kk': 0.85 + nrm((L, RWKV_DIM), 0.02),
        'rwkv_ka': near_one((L, RWKV_DIM)),
        'rwkv_rk': nrm((L, RWKV_HEADS, RWKV_HEAD), 0.1) - 0.04,
        'rwkv_lnx_g': near_one((L, RWKV_DIM)),
        'rwkv_lnx_b': nrm((L, RWKV_DIM), 0.02),
        'hgrn_lb': near_one((L, HGRN_DIM), 0.1),
        'hgrn_norm_g': near_one((L, HGRN_DV)),
        'w_branch': nrm((L, N_BRANCH, CONV_DIM, D_MODEL), CONV_DIM ** -0.5 * DN_BETA),
        'b_gate': nrm((L, N_BRANCH, D_MODEL), 0.01),
        'w_out': nrm((L, D_MODEL, D_MODEL), D_MODEL ** -0.5 * DN_BETA),
        'ln1_g': near_one((L, D_MODEL)),
        'ln1_b': nrm((L, D_MODEL), 0.02),
        'ffn_wg': nrm((n_dense, D_MODEL, D_FF), D_MODEL ** -0.5),
        'ffn_wu': nrm((n_dense, D_MODEL, D_FF), D_MODEL ** -0.5 * DN_BETA),
        'ffn_wd': nrm((n_dense, D_FF, D_MODEL), D_FF ** -0.5 * DN_BETA),
        'moe_router': nrm((n_moe, D_MODEL, N_EXPERTS), D_MODEL ** -0.5),
        'moe_wg': nrm((n_moe, N_EXPERTS, D_MODEL, D_FF), D_MODEL ** -0.5),
        'moe_wu': nrm((n_moe, N_EXPERTS, D_MODEL, D_FF), D_MODEL ** -0.5 * DN_BETA),
        'moe_wd': nrm((n_moe, N_EXPERTS, D_FF, D_MODEL), D_FF ** -0.5 * DN_BETA),
        'ple_wp': nrm((L, PLE_DIM, D_MODEL), PLE_DIM ** -0.5 * DN_BETA),
        'ple_wg': nrm((L, D_MODEL, D_MODEL), D_MODEL ** -0.5),
        'ple_bg': nrm((L, D_MODEL), 0.01),
        'ln2_g': near_one((L, D_MODEL)),
        'ln2_b': nrm((L, D_MODEL), 0.02),
    }


def reference(x, p, w_in, w_in_vres, conv_dw, conv_dw_b, conv_ln_g, conv_ln_b, rwkv_mu,
              rwkv_mu_vres, rwkv_w0, rwkv_w2, rwkv_a0, rwkv_a2, rwkv_v0, rwkv_v2, rwkv_g2,
              rwkv_kk, rwkv_ka, rwkv_rk, rwkv_lnx_g, rwkv_lnx_b, hgrn_lb, hgrn_norm_g,
              w_branch, b_gate, w_out, ln1_g, ln1_b, ffn_wg, ffn_wu, ffn_wd, moe_router,
              moe_wg, moe_wu, moe_wd, ple_wp, ple_wg, ple_bg, ln2_g, ln2_b):
    f32 = jnp.float32
    bsz, seq, _ = x.shape
    lb_soft = jax.nn.softmax(hgrn_lb.astype(f32), axis=0)
    lower_bounds = jnp.cumsum(lb_soft, axis=0) - lb_soft[0]
    h = x
    v_first = None
    for i in range(DEPTH):
        vres = i > 0
        w_comb = jnp.concatenate([w_in[i], w_in_vres[i - 1]], axis=1) if vres else w_in[i]
        z = h @ w_comb
        parts = _split(z, [2 * CONV_DIM, RWKV_BASE, 4 * HGRN_DIM, N_BRANCH * D_MODEL]
                       + ([MV_LORA] if vres else []))
        z_conv, z_rwkv, z_hgrn, z_gate = parts[:4]
        if vres:
            z_rwkv = jnp.concatenate([z_rwkv, parts[4]], axis=-1)
            mu = jnp.concatenate([rwkv_mu[i], rwkv_mu_vres[i - 1]], axis=-1)
        else:
            mu = rwkv_mu[i]
        z_rwkv = z_rwkv.astype(f32)
        z_rwkv = z_rwkv + (_token_shift(z_rwkv) - z_rwkv) * mu

        y_conv = conformer_conv(z_conv.astype(f32), conv_dw[i], conv_dw_b[i], conv_ln_g[i], conv_ln_b[i])
        y_rwkv, v_new = rwkv7_time_mix(
            z_rwkv, v_first, rwkv_w0[i], rwkv_w2[i], rwkv_a0[i], rwkv_a2[i], rwkv_g2[i],
            rwkv_kk[i], rwkv_ka[i], rwkv_rk[i], rwkv_lnx_g[i], rwkv_lnx_b[i],
            rwkv_v0[i - 1] if vres else None, rwkv_v2[i - 1] if vres else None)
        if not vres:
            v_first = v_new
        y_hgrn = hgrn2_mix(z_hgrn.astype(f32), lower_bounds[i], hgrn_norm_g[i])

        gates = jax.nn.sigmoid(z_gate.reshape(bsz, seq, N_BRANCH, D_MODEL) + b_gate[i])
        merged = jnp.zeros_like(h)
        for n, y_b in enumerate((y_conv, y_rwkv, y_hgrn)):
            merged = merged + gates[:, :, n] * (y_b.astype(h.dtype) @ w_branch[i, n])
        h = layer_norm(DN_ALPHA * h + merged @ w_out[i], ln1_g[i], ln1_b[i])

        if i % 2 == 0:
            j = i // 2
            ffn = swiglu(h, ffn_wg[j], ffn_wu[j], ffn_wd[j])
        else:
            j = i // 2
            ffn = moe_swiglu(h, moe_router[j], moe_wg[j], moe_wu[j], moe_wd[j])
        ple = jax.nn.sigmoid(h @ ple_wg[i] + ple_bg[i]) * (p[i] @ ple_wp[i])
        h = layer_norm(DN_ALPHA * h + ffn + ple, ln2_g[i], ln2_b[i])
    return h
```

```python
import functools

import jax
import jax.numpy as jnp
from jax import lax
from jax.experimental import pallas as pl
from jax.experimental.pallas import tpu as pltpu

F32 = jnp.float32
BF16 = jnp.bfloat16
HI = lax.Precision.HIGHEST

V7X_VMEM_LIMIT_BYTES = 48 * 1024 * 1024

LN_EPS = 1e-5
RWKV_LN_EPS = 64e-5
RWKV_HEAD = 64
HGRN_EXPAND = 128
CHUNK = 64
CONV_HALO = 32
N_EXPERTS = 8
TOP_K = 2

NT_DIMS = (((1,), (1,)), ((), ()))
TN_DIMS = (((0,), (0,)), ((), ()))


def _params(*semantics):
    return pltpu.CompilerParams(dimension_semantics=semantics,
                                vmem_limit_bytes=V7X_VMEM_LIMIT_BYTES)


def _sigmoid(z):
    return 1.0 / (1.0 + jnp.exp(-z))


def _silu(z):
    return z * _sigmoid(z)


def _act(z, kind):
    return _sigmoid(z) if kind == "sigmoid" else _silu(z)


def _layer_norm(x, g, b, eps):
    mu = jnp.mean(x, axis=-1, keepdims=True)
    xc = x - mu
    var = jnp.mean(xc * xc, axis=-1, keepdims=True)
    return xc * lax.rsqrt(var + eps) * g + b


def _mm_kernel(x_ref, w_ref, o_ref):
    o_ref[...] = jnp.dot(x_ref[...], w_ref[...], preferred_element_type=F32).astype(o_ref.dtype)


def _matmul(x, w, *, out_dtype, tm, tn):
    m, k = x.shape
    n = w.shape[1]
    return pl.pallas_call(
        _mm_kernel,
        grid=(m // tm, n // tn),
        in_specs=[pl.BlockSpec((tm, k), lambda i, j: (i, 0)),
                  pl.BlockSpec((k, tn), lambda i, j: (0, j))],
        out_specs=pl.BlockSpec((tm, tn), lambda i, j: (i, j)),
        out_shape=jax.ShapeDtypeStruct((m, n), out_dtype),
        compiler_params=_params("parallel", "arbitrary"),
        name="matmul",
    )(x, w)


def _gated_mm_body(x1_ref, w1_ref, b1_ref, x2_refs, w2_ref, o_ref, *, nb, act):
    a = x1_ref[...]
    acc = None
    for n in range(nb):
        z = jnp.dot(a, w1_ref[n], preferred_element_type=F32)
        if b1_ref is not None:
            z = z + b1_ref[n]
        lhs2 = a if x2_refs is None else x2_refs[n][...]
        lin = jnp.dot(lhs2, w2_ref[n], preferred_element_type=F32)
        t = _act(z, act) * lin
        acc = t if acc is None else acc + t
    o_ref[...] = acc.astype(o_ref.dtype)


def _gated_mm_kernel(*refs, nb, act, has_bias, n_x2):
    refs = list(refs)
    x1_ref, w1_ref = refs[0], refs[1]
    pos = 2
    b1_ref = None
    if has_bias:
        b1_ref = refs[pos]
        pos += 1
    x2_refs = refs[pos:pos + n_x2] if n_x2 else None
    pos += n_x2
    w2_ref, o_ref = refs[pos], refs[pos + 1]
    _gated_mm_body(x1_ref, w1_ref, b1_ref, x2_refs, w2_ref, o_ref, nb=nb, act=act)


def _gated_mm(x1, w1, w2, *, x2s=None, b1=None, act, out_dtype, tm, tn):
    m, k1 = x1.shape
    nb, _, n = w1.shape
    k2 = w2.shape[1]
    in_specs = [pl.BlockSpec((tm, k1), lambda i, j: (i, 0)),
                pl.BlockSpec((nb, k1, tn), lambda i, j: (0, 0, j))]
    args = [x1, w1]
    if b1 is not None:
        in_specs.append(pl.BlockSpec((nb, 1, tn), lambda i, j: (0, 0, j)))
        args.append(b1)
    n_x2 = 0
    if x2s is not None:
        n_x2 = len(x2s)
        for x2 in x2s:
            in_specs.append(pl.BlockSpec((tm, k2), lambda i, j: (i, 0)))
            args.append(x2)
    in_specs.append(pl.BlockSpec((nb, k2, tn), lambda i, j: (0, 0, j)))
    args.append(w2)
    return pl.pallas_call(
        functools.partial(_gated_mm_kernel, nb=nb, act=act, has_bias=b1 is not None, n_x2=n_x2),
        grid=(m // tm, n // tn),
        in_specs=in_specs,
        out_specs=pl.BlockSpec((tm, tn), lambda i, j: (i, j)),
        out_shape=jax.ShapeDtypeStruct((m, n), out_dtype),
        compiler_params=_params("parallel", "arbitrary"),
        name="gated_matmul",
    )(*args)


def _grouped_gated_kernel(te_ref, nt_ref, x_ref, w1_ref, w2_ref, o_ref, *, act):
    i = pl.program_id(0)

    @pl.when(i < nt_ref[0])
    def _():
        a = x_ref[...]
        z = jnp.dot(a, w1_ref[0], preferred_element_type=F32)
        lin = jnp.dot(a, w2_ref[0], preferred_element_type=F32)
        o_ref[...] = (_act(z, act) * lin).astype(o_ref.dtype)

    @pl.when(i >= nt_ref[0])
    def _():
        o_ref[...] = jnp.zeros_like(o_ref)


def _grouped_gated_mm(x, w1, w2, tile_expert, n_tiles, *, act, out_dtype, tm, tn):
    m, k = x.shape
    n = w1.shape[2]
    grid_spec = pltpu.PrefetchScalarGridSpec(
        num_scalar_prefetch=2,
        grid=(m // tm, n // tn),
        in_specs=[pl.BlockSpec((tm, k), lambda i, j, te, nt: (i, 0)),
                  pl.BlockSpec((1, k, tn), lambda i, j, te, nt: (te[i], 0, j)),
                  pl.BlockSpec((1, k, tn), lambda i, j, te, nt: (te[i], 0, j))],
        out_specs=pl.BlockSpec((tm, tn), lambda i, j, te, nt: (i, j)),
    )
    return pl.pallas_call(
        functools.partial(_grouped_gated_kernel, act=act),
        grid_spec=grid_spec,
        out_shape=jax.ShapeDtypeStruct((m, n), out_dtype),
        compiler_params=_params("arbitrary", "arbitrary"),
        name="grouped_gated_matmul",
    )(tile_expert, n_tiles, x, w1, w2)


def _mm_acc_kernel(x_ref, w_ref, o_ref):
    k = pl.program_id(1)
    part = jnp.dot(x_ref[...], w_ref[...], preferred_element_type=F32)

    @pl.when(k == 0)
    def _():
        o_ref[...] = part

    @pl.when(k > 0)
    def _():
        o_ref[...] += part


def _matmul_ktiled(x, w, *, tm, tk):
    m, kdim = x.shape
    n = w.shape[1]
    return pl.pallas_call(
        _mm_acc_kernel,
        grid=(m // tm, kdim // tk),
        in_specs=[pl.BlockSpec((tm, tk), lambda i, k: (i, k)),
                  pl.BlockSpec((tk, n), lambda i, k: (k, 0))],
        out_specs=pl.BlockSpec((tm, n), lambda i, k: (i, 0)),
        out_shape=jax.ShapeDtypeStruct((m, n), F32),
        compiler_params=_params("parallel", "arbitrary"),
        name="matmul_ktiled",
    )(x, w)


def _grouped_mm_acc_kernel(te_ref, nt_ref, x_ref, w_ref, o_ref):
    i = pl.program_id(0)
    k = pl.program_id(1)

    @pl.when(i < nt_ref[0])
    def _():
        part = jnp.dot(x_ref[...], w_ref[0], preferred_element_type=F32)

        @pl.when(k == 0)
        def _():
            o_ref[...] = part

        @pl.when(k > 0)
        def _():
            o_ref[...] += part

    @pl.when(jnp.logical_and(i >= nt_ref[0], k == 0))
    def _():
        o_ref[...] = jnp.zeros_like(o_ref)


def _grouped_matmul_ktiled(x, w, tile_expert, n_tiles, *, tm, tk):
    m, kdim = x.shape
    n = w.shape[2]
    grid_spec = pltpu.PrefetchScalarGridSpec(
        num_scalar_prefetch=2,
        grid=(m // tm, kdim // tk),
        in_specs=[pl.BlockSpec((tm, tk), lambda i, k, te, nt: (i, k)),
                  pl.BlockSpec((1, tk, n), lambda i, k, te, nt: (te[i], k, 0))],
        out_specs=pl.BlockSpec((tm, n), lambda i, k, te, nt: (i, 0)),
    )
    return pl.pallas_call(
        _grouped_mm_acc_kernel,
        grid_spec=grid_spec,
        out_shape=jax.ShapeDtypeStruct((m, n), F32),
        compiler_params=_params("arbitrary", "arbitrary"),
        name="grouped_matmul_ktiled",
    )(tile_expert, n_tiles, x, w)


def _proj_res_ln_kernel(h_ref, m_ref, w_ref, g_ref, b_ref, o_ref, obf_ref, *, alpha):
    y = alpha * h_ref[...] + jnp.dot(m_ref[...], w_ref[...], preferred_element_type=F32)
    out = _layer_norm(y, g_ref[...], b_ref[...], LN_EPS)
    o_ref[...] = out
    obf_ref[...] = out.astype(BF16)


def _proj_res_ln(h, m_in, w, g, b, *, alpha, tm):
    n, d = h.shape
    k = m_in.shape[1]
    return pl.pallas_call(
        functools.partial(_proj_res_ln_kernel, alpha=alpha),
        grid=(n // tm,),
        in_specs=[pl.BlockSpec((tm, d), lambda i: (i, 0)),
                  pl.BlockSpec((tm, k), lambda i: (i, 0)),
                  pl.BlockSpec((k, d), lambda i: (0, 0)),
                  pl.BlockSpec((1, d), lambda i: (0, 0)),
                  pl.BlockSpec((1, d), lambda i: (0, 0))],
        out_specs=[pl.BlockSpec((tm, d), lambda i: (i, 0)),
                   pl.BlockSpec((tm, d), lambda i: (i, 0))],
        out_shape=[jax.ShapeDtypeStruct((n, d), F32), jax.ShapeDtypeStruct((n, d), BF16)],
        compiler_params=_params("parallel"),
        name="proj_res_ln",
    )(h, m_in, w, g, b)


def _add_ln_kernel(*refs, alpha, n_terms, weighted):
    h_ref = refs[0]
    a_refs = refs[1:1 + n_terms]
    pos = 1 + n_terms
    c_ref = None
    if weighted:
        c_ref = refs[pos]
        pos += 1
    ple_ref, g_ref, b_ref, o_ref, obf_ref = refs[pos:pos + 5]
    y = alpha * h_ref[...] + ple_ref[...]
    for j in range(n_terms):
        t = a_refs[j][...]
        if weighted:
            t = t * c_ref[:, j:j + 1]
        y = y + t
    out = _layer_norm(y, g_ref[...], b_ref[...], LN_EPS)
    o_ref[...] = out
    obf_ref[...] = out.astype(BF16)


def _add_ln(h, terms, coef, ple, g, b, *, alpha, tm):
    n, d = h.shape
    row = pl.BlockSpec((tm, d), lambda i: (i, 0))
    vec = pl.BlockSpec((1, d), lambda i: (0, 0))
    in_specs = [row] + [row] * len(terms)
    args = [h] + list(terms)
    if coef is not None:
        in_specs.append(pl.BlockSpec((tm, coef.shape[1]), lambda i: (i, 0)))
        args.append(coef)
    in_specs += [row, vec, vec]
    args += [ple, g, b]
    return pl.pallas_call(
        functools.partial(_add_ln_kernel, alpha=alpha, n_terms=len(terms), weighted=coef is not None),
        grid=(n // tm,),
        in_specs=in_specs,
        out_specs=[row, row],
        out_shape=[jax.ShapeDtypeStruct((n, d), F32), jax.ShapeDtypeStruct((n, d), BF16)],
        compiler_params=_params("parallel"),
        name="add_ln",
    )(*args)


def _conv_kernel(u_ref, dw_ref, dwb_ref, g_ref, b_ref, o_ref, buf, *, ts, rb, width):
    s = pl.program_id(1)

    @pl.when(s == 0)
    def _():
        buf[0:CONV_HALO, :] = jnp.zeros((CONV_HALO, buf.shape[1]), F32)

    buf[CONV_HALO:CONV_HALO + ts, :] = u_ref[0]
    lead = CONV_HALO - (width - 1)
    for r0 in range(0, ts, rb):
        acc = jnp.broadcast_to(dwb_ref[...], (rb, buf.shape[1]))
        for j in range(width):
            acc = acc + buf[r0 + lead + j:r0 + lead + j + rb, :] * dw_ref[j:j + 1, :]
        y = _silu(_layer_norm(acc, g_ref[...], b_ref[...], LN_EPS))
        o_ref[0, r0:r0 + rb, :] = y.astype(o_ref.dtype)
    buf[0:CONV_HALO, :] = buf[ts:ts + CONV_HALO, :]


def _conv_module(u, dw, dw_b, ln_g, ln_b, *, ts, rb):
    bsz, seq, c = u.shape
    width = dw.shape[0]
    dw_pad = jnp.zeros((CONV_HALO, c), F32).at[:width].set(dw)
    vec = pl.BlockSpec((1, c), lambda b, s: (0, 0))
    return pl.pallas_call(
        functools.partial(_conv_kernel, ts=ts, rb=rb, width=width),
        grid=(bsz, seq // ts),
        in_specs=[pl.BlockSpec((1, ts, c), lambda b, s: (b, s, 0)),
                  pl.BlockSpec((CONV_HALO, c), lambda b, s: (0, 0)),
                  vec, vec, vec],
        out_specs=pl.BlockSpec((1, ts, c), lambda b, s: (b, s, 0)),
        out_shape=jax.ShapeDtypeStruct((bsz, seq, c), BF16),
        scratch_shapes=[pltpu.VMEM((CONV_HALO + ts, c), F32)],
        compiler_params=_params("parallel", "arbitrary"),
        name="conv_module",
    )(u, dw_pad, dw_b.reshape(1, c), ln_g.reshape(1, c), ln_b.reshape(1, c))


def _hgrn_kernel(zq_ref, zf_ref, zi_ref, zg_ref, lb_ref, ng_ref, o_ref, state_t, k_buf, b_buf, *, ts):
    s = pl.program_id(2)

    @pl.when(s == 0)
    def _():
        state_t[...] = jnp.zeros_like(state_t)

    row = lax.broadcasted_iota(jnp.int32, (CHUNK, CHUNK), 0)
    col = lax.broadcasted_iota(jnp.int32, (CHUNK, CHUNK), 1)
    causal = row >= col
    tril_ones = causal.astype(F32)
    for c in range(ts // CHUNK):
        rs = slice(c * CHUNK, (c + 1) * CHUNK)
        q = _silu(zq_ref[0, rs, :]) * (HGRN_EXPAND ** -0.5)
        k = (1.0 - lb_ref[...]) * _sigmoid(-zf_ref[0, rs, :])
        log_f = jnp.log(1.0 - k)
        b = jnp.dot(tril_ones, log_f, precision=HI, preferred_element_type=F32)
        v = zi_ref[0, rs, :]
        k_buf[...] = k
        b_buf[...] = b

        def column(sj, att):
            k_row = k_buf[pl.ds(sj, 1), :]
            b_row = b_buf[pl.ds(sj, 1), :]
            e = jnp.exp(jnp.minimum(b - b_row, 0.0))
            colv = jnp.sum(q * (k_row * e), axis=-1, keepdims=True)
            return jnp.where(col == sj, colv, att)

        att = lax.fori_loop(0, CHUNK, column, jnp.zeros((CHUNK, CHUNK), F32), unroll=4)
        att = jnp.where(causal, att, 0.0)
        st = state_t[...]
        o = (jnp.dot(att, v, precision=HI, preferred_element_type=F32)
             + lax.dot_general(q * jnp.exp(b), st, NT_DIMS, precision=HI, preferred_element_type=F32))
        b_last = b[CHUNK - 1:CHUNK, :]
        k_hat = k * jnp.exp(b_last - b)
        state_t[...] = (st * jnp.exp(b_last)
                        + lax.dot_general(v, k_hat, TN_DIMS, precision=HI, preferred_element_type=F32))
        o = o * lax.rsqrt(jnp.mean(o * o, axis=-1, keepdims=True) + LN_EPS) * ng_ref[...]
        o_ref[0, rs, :] = (o * _silu(zg_ref[0, rs, :])).astype(o_ref.dtype)


def _hgrn_mix(z, lb, norm_g, *, ts):
    bsz, seq, c4 = z.shape
    dh = HGRN_EXPAND
    heads = c4 // (4 * dh)

    def zspec(group):
        return pl.BlockSpec((1, ts, dh), lambda b, h, s: (b, s, group * heads + h))

    return pl.pallas_call(
        functools.partial(_hgrn_kernel, ts=ts),
        grid=(bsz, heads, seq // ts),
        in_specs=[zspec(0), zspec(1), zspec(2), zspec(3),
                  pl.BlockSpec((1, dh), lambda b, h, s: (0, h)),
                  pl.BlockSpec((1, dh), lambda b, h, s: (0, 0))],
        out_specs=pl.BlockSpec((1, ts, dh), lambda b, h, s: (b, s, h)),
        out_shape=jax.ShapeDtypeStruct((bsz, seq, heads * dh), BF16),
        scratch_shapes=[pltpu.VMEM((dh, dh), F32),
                        pltpu.VMEM((CHUNK, dh), F32),
                        pltpu.VMEM((CHUNK, dh), F32)],
        compiler_params=_params("parallel", "parallel", "arbitrary"),
        name="hgrn_mix",
    )(z, z, z, z, lb.reshape(1, heads * dh), norm_g.reshape(1, dh))


def _token_shift(z, prev_row, mu):
    rows = lax.broadcasted_iota(jnp.int32, z.shape, 0)
    shifted = jnp.where(rows == 0, prev_row, pltpu.roll(z, 1, axis=0))
    return z + (shifted - z) * mu


def _rwkv_prep_kernel(*refs, vres, c):
    (zrkv_ref, zl_ref, mu_rkv_ref, mu_l_ref, w0_ref, w2_ref, a0_ref, a2_ref, g2_ref,
     kk_ref, ka_ref) = refs[:11]
    pos = 11
    if vres:
        v0_ref, v2_ref, vfirst_ref = refs[pos:pos + 3]
        pos += 3
    r_o, k_o, v_o, ld_o, kk_o, a_o, g_o, prev_rkv, prev_l = refs[pos:pos + 9]
    s = pl.program_id(1)

    @pl.when(s == 0)
    def _():
        prev_rkv[...] = jnp.zeros_like(prev_rkv)
        prev_l[...] = jnp.zeros_like(prev_l)

    z = zrkv_ref[0]
    zl = zl_ref[0]
    ts = z.shape[0]
    z_mix = _token_shift(z, prev_rkv[...], mu_rkv_ref[...])
    zl_mix = _token_shift(zl, prev_l[...], mu_l_ref[...])
    prev_rkv[...] = z[ts - 1:ts, :]
    prev_l[...] = zl[ts - 1:ts, :]

    zr, zk, zv = z_mix[:, 0:c], z_mix[:, c:2 * c], z_mix[:, 2 * c:3 * c]
    zw, za, zg = zl_mix[:, 0:128], zl_mix[:, 128:256], zl_mix[:, 256:512]

    def lora(act_in, w_ref):
        return jnp.dot(act_in.astype(BF16), w_ref[...], preferred_element_type=F32)

    x = -(w0_ref[...] + lora(jnp.tanh(zw), w2_ref))
    softplus = jnp.maximum(x, 0.0) + jnp.log(1.0 + jnp.exp(-jnp.abs(x)))
    w = -softplus - 0.5
    a = _sigmoid(a0_ref[...] + lora(za, a2_ref))
    g = lora(_sigmoid(zg), g2_ref)
    v = zv
    if vres:
        zm = zl_mix[:, 512:640]
        v = zv + (vfirst_ref[0] - zv) * _sigmoid(v0_ref[...] + lora(zm, v2_ref))
    r_o[0] = zr
    k_o[0] = zk * (1.0 + (a - 1.0) * ka_ref[...])
    v_o[0] = v
    ld_o[0] = -jnp.exp(w)
    kk_o[0] = zk * kk_ref[...]
    a_o[0] = a
    g_o[0] = g


def _rwkv_prep(z_rkv, z_lora, mu_rkv, mu_l, w0, w2, a0, a2, g2, k_k, k_a, v0, v2, v_first, *, ts):
    bsz, seq, c3 = z_rkv.shape
    c = c3 // 3
    lw = z_lora.shape[2]
    vres = v0 is not None

    def full(arr):
        return pl.BlockSpec(arr.shape, lambda b, s: (0,) * arr.ndim)

    tile = pl.BlockSpec((1, ts, c), lambda b, s: (b, s, 0))
    args = [z_rkv, z_lora, mu_rkv, mu_l, w0, w2, a0, a2, g2, k_k, k_a]
    in_specs = [pl.BlockSpec((1, ts, c3), lambda b, s: (b, s, 0)),
                pl.BlockSpec((1, ts, lw), lambda b, s: (b, s, 0))] + [full(a) for a in args[2:]]
    if vres:
        args += [v0, v2, v_first]
        in_specs += [full(v0), full(v2), tile]
    return pl.pallas_call(
        functools.partial(_rwkv_prep_kernel, vres=vres, c=c),
        grid=(bsz, seq // ts),
        in_specs=in_specs,
        out_specs=[tile] * 7,
        out_shape=[jax.ShapeDtypeStruct((bsz, seq, c), F32)] * 7,
        scratch_shapes=[pltpu.VMEM((1, c3), F32), pltpu.VMEM((1, lw), F32)],
        compiler_params=_params("parallel", "arbitrary"),
        name="rwkv_prep",
    )(*args)


def _rwkv_scan_kernel(r_ref, k_ref, v_ref, ld_ref, kk_ref, a_ref, g_ref, rk_ref, lng_ref, lnb_ref,
                      o_ref, state, *, ts, hg):
    s = pl.program_id(2)
    n = RWKV_HEAD

    @pl.when(s == 0)
    def _():
        state[...] = jnp.zeros_like(state)

    row = lax.broadcasted_iota(jnp.int32, (CHUNK, CHUNK), 0)
    col = lax.broadcasted_iota(jnp.int32, (CHUNK, CHUNK), 1)
    incl = row >= col
    strict = row > col
    tril_ones = incl.astype(F32)
    eye = (row == col).astype(F32)

    def mm(a, b):
        return jnp.dot(a, b, precision=HI, preferred_element_type=F32)

    for c in range(ts // CHUNK):
        rs = slice(c * CHUNK, (c + 1) * CHUNK)
        outs = []
        for hh in range(hg):
            cs = slice(hh * n, (hh + 1) * n)
            r = r_ref[0, rs, cs]
            k = k_ref[0, rs, cs]
            v = v_ref[0, rs, cs]
            ld = ld_ref[0, rs, cs]
            kk = kk_ref[0, rs, cs]
            a = a_ref[0, rs, cs]
            lp = mm(tril_ones, ld)
            lp_prev = lp - ld
            norm = jnp.sqrt(jnp.sum(kk * kk, axis=-1, keepdims=True))
            kkn = kk / jnp.maximum(norm, 1e-12)
            bvec = kkn * a
            inv_p = jnp.exp(-lp)
            a_t = -kkn * jnp.exp(lp_prev)
            r_t = r * jnp.exp(lp)
            b_t = bvec * inv_p
            k_t = k * inv_p
            lp_last = lp[CHUNK - 1:CHUNK, :]
            to_end = jnp.exp(lp_last - lp)
            b_h = bvec * to_end
            k_h = k * to_end

            ar = jnp.concatenate([a_t, r_t], axis=0)
            bk = jnp.concatenate([b_t, k_t], axis=0)
            gram = lax.dot_general(ar, bk, NT_DIMS, precision=HI, preferred_element_type=F32)
            t_ab = jnp.where(strict, gram[0:CHUNK, 0:CHUNK], 0.0)
            t_ak = jnp.where(strict, gram[0:CHUNK, CHUNK:], 0.0)
            t_rb = jnp.where(incl, gram[CHUNK:, 0:CHUNK], 0.0)
            t_rk = jnp.where(incl, gram[CHUNK:, CHUNK:], 0.0)

            inv = eye + t_ab
            power = t_ab
            steps = CHUNK.bit_length() - 2
            for _ in range(steps):
                power = mm(power, power)
                inv = inv + mm(inv, power)

            st = state[hh]
            ar_s = lax.dot_general(ar, st, NT_DIMS, precision=HI, preferred_element_type=F32)
            u = mm(inv, ar_s[0:CHUNK] + mm(t_ak, v))
            y = ar_s[CHUNK:] + mm(t_rb, u) + mm(t_rk, v)
            state[hh] = (st * jnp.exp(lp_last)
                         + lax.dot_general(u, b_h, TN_DIMS, precision=HI, preferred_element_type=F32)
                         + lax.dot_general(v, k_h, TN_DIMS, precision=HI, preferred_element_type=F32))

            mu = jnp.mean(y, axis=-1, keepdims=True)
            yc = y - mu
            var = jnp.mean(yc * yc, axis=-1, keepdims=True)
            yn = yc * lax.rsqrt(var + RWKV_LN_EPS) * lng_ref[:, cs] + lnb_ref[:, cs]
            bonus = jnp.sum(r * k * rk_ref[:, cs], axis=-1, keepdims=True) * v
            outs.append((yn + bonus) * g_ref[0, rs, cs])
        o_ref[0, rs, :] = jnp.concatenate(outs, axis=-1).astype(o_ref.dtype)


def _rwkv_scan(r, k, v, ld, kk, a, g, r_k, lnx_g, lnx_b, *, ts, hg):
    bsz, seq, c = r.shape
    wblk = hg * RWKV_HEAD
    tile = pl.BlockSpec((1, ts, wblk), lambda b, h, s: (b, s, h))
    vec = pl.BlockSpec((1, wblk), lambda b, h, s: (0, h))
    return pl.pallas_call(
        functools.partial(_rwkv_scan_kernel, ts=ts, hg=hg),
        grid=(bsz, c // wblk, seq // ts),
        in_specs=[tile] * 7 + [vec] * 3,
        out_specs=tile,
        out_shape=jax.ShapeDtypeStruct((bsz, seq, c), BF16),
        scratch_shapes=[pltpu.VMEM((hg, RWKV_HEAD, RWKV_HEAD), F32)],
        compiler_params=_params("parallel", "parallel", "arbitrary"),
        name="rwkv_scan",
    )(r, k, v, ld, kk, a, g, r_k.reshape(1, c), lnx_g.reshape(1, c), lnx_b.reshape(1, c))


def _router_kernel(h_ref, w_ref, idx_ref, wt_ref, *, n_experts):
    logits = jnp.dot(h_ref[...], w_ref[...], precision=HI, preferred_element_type=F32)
    lane = lax.broadcasted_iota(jnp.int32, logits.shape, 1)
    neg = jnp.float32(-jnp.inf)
    big = jnp.int32(logits.shape[1])
    x = jnp.where(lane < n_experts, logits, neg)
    m1 = jnp.max(x, axis=-1, keepdims=True)
    i1 = jnp.min(jnp.where(x == m1, lane, big), axis=-1, keepdims=True)
    x2 = jnp.where(lane == i1, neg, x)
    m2 = jnp.max(x2, axis=-1, keepdims=True)
    i2 = jnp.min(jnp.where(x2 == m2, lane, big), axis=-1, keepdims=True)
    e2 = jnp.exp(m2 - m1)
    w1 = 1.0 / (1.0 + e2)
    w2 = e2 / (1.0 + e2)
    idx_ref[...] = jnp.where(lane == 0, i1, jnp.where(lane == 1, i2, 0))
    wt_ref[...] = jnp.where(lane == 0, w1, jnp.where(lane == 1, w2, 0.0))


def _router(h, w_router, *, tm):
    n, d = h.shape
    n_experts = w_router.shape[1]
    w_pad = jnp.zeros((d, 128), F32).at[:, :n_experts].set(w_router)
    out = pl.BlockSpec((tm, 128), lambda i: (i, 0))
    return pl.pallas_call(
        functools.partial(_router_kernel, n_experts=n_experts),
        grid=(n // tm,),
        in_specs=[pl.BlockSpec((tm, d), lambda i: (i, 0)),
                  pl.BlockSpec((d, 128), lambda i: (0, 0))],
        out_specs=[out, out],
        out_shape=[jax.ShapeDtypeStruct((n, 128), jnp.int32), jax.ShapeDtypeStruct((n, 128), F32)],
        compiler_params=_params("parallel"),
        name="moe_router",
    )(h, w_pad)


def _moe(h, h_bf, w_router, wg, wu, wd, *, tm):
    n, d = h.shape
    n_experts = wg.shape[0]
    idx_pad, wt_pad = _router(h, w_router, tm=512)
    e_flat = idx_pad[:, :TOP_K].reshape(-1)
    onehot = (e_flat[:, None] == jnp.arange(n_experts, dtype=jnp.int32)[None, :]).astype(jnp.int32)
    csum = jnp.cumsum(onehot, axis=0)
    rank = jnp.sum(csum * onehot, axis=1) - 1
    counts = csum[-1]
    padded = ((counts + tm - 1) // tm) * tm
    ends = jnp.cumsum(padded)
    starts = ends - padded
    pos = starts[e_flat] + rank
    n_rows = TOP_K * n + n_experts * tm
    src = jnp.zeros((n_rows,), jnp.int32).at[pos].set(jnp.arange(TOP_K * n, dtype=jnp.int32) // TOP_K)
    n_tiles = (ends[-1] // tm).astype(jnp.int32).reshape(1)
    tile_start = jnp.arange(n_rows // tm, dtype=jnp.int32) * tm
    tile_expert = jnp.minimum(jnp.searchsorted(ends, tile_start, side="right"), n_experts - 1).astype(jnp.int32)
    x_sorted = h_bf.at[src].get(mode="promise_in_bounds")
    h1 = _grouped_gated_mm(x_sorted, wg, wu, tile_expert, n_tiles, act="silu", out_dtype=BF16, tm=tm, tn=1024)
    y_sorted = _grouped_matmul_ktiled(h1, wd, tile_expert, n_tiles, tm=tm, tk=1024)
    pos2 = pos.reshape(n, TOP_K)
    y0 = y_sorted.at[pos2[:, 0]].get(mode="promise_in_bounds")
    y1 = y_sorted.at[pos2[:, 1]].get(mode="promise_in_bounds")
    return [y0, y1], wt_pad


def _pad_cols(w, width):
    return jnp.pad(w, ((0, 0), (0, width - w.shape[1])))


def _pad_rows(w, height):
    return jnp.pad(w, ((0, height - w.shape[0]), (0, 0)))


def kernel(x, p, w_in, w_in_vres, conv_dw, conv_dw_b, conv_ln_g, conv_ln_b, rwkv_mu, rwkv_mu_vres, rwkv_w0, rwkv_w2, rwkv_a0, rwkv_a2, rwkv_v0, rwkv_v2, rwkv_g2, rwkv_kk, rwkv_ka, rwkv_rk, rwkv_lnx_g, rwkv_lnx_b, hgrn_lb, hgrn_norm_g, w_branch, b_gate, w_out, ln1_g, ln1_b, ffn_wg, ffn_wu, ffn_wd, moe_router, moe_wg, moe_wu, moe_wd, ple_wp, ple_wg, ple_bg, ln2_g, ln2_b):
    bsz, seq, d = x.shape
    depth = w_in.shape[0]
    n = bsz * seq
    cc = conv_dw.shape[2]
    cr = rwkv_w0.shape[1]
    ch = hgrn_lb.shape[1]
    d_decay, d_aaa, d_gate, d_mv = rwkv_w2.shape[1], rwkv_a2.shape[1], rwkv_g2.shape[1], rwkv_v2.shape[1]
    n_branch = w_branch.shape[1]
    alpha = float((2 * depth) ** 0.25)
    tm = min(1024, n)
    ts = min(256, seq)

    lb_soft = jax.nn.softmax(hgrn_lb.astype(F32), axis=0)
    lower_bounds = jnp.cumsum(lb_soft, axis=0) - lb_soft[0]

    h = x.reshape(n, d)
    h_bf = h.astype(BF16)
    v_first = None
    for i in range(depth):
        vres = i > 0
        wi = w_in[i]
        o_rwkv = 2 * cc
        o_lora = o_rwkv + 3 * cr
        o_hgrn = o_lora + d_decay + d_aaa + d_gate
        o_gate = o_hgrn + 4 * ch

        w_val = wi[:, 0:cc].astype(BF16)[None]
        w_glu = wi[:, cc:2 * cc].astype(BF16)[None]
        u = _gated_mm(h_bf, w_glu, w_val, act="sigmoid", out_dtype=F32, tm=tm, tn=min(1024, cc))
        y_conv = _conv_module(u.reshape(bsz, seq, cc), conv_dw[i], conv_dw_b[i], conv_ln_g[i], conv_ln_b[i],
                              ts=ts, rb=16).reshape(n, cc)

        z_rkv = _matmul(h_bf, wi[:, o_rwkv:o_lora].astype(BF16), out_dtype=F32, tm=tm, tn=min(1024, cr))
        lora_cols = [_pad_cols(wi[:, o_lora:o_lora + d_decay], 128),
                     _pad_cols(wi[:, o_lora + d_decay:o_lora + d_decay + d_aaa], 128),
                     _pad_cols(wi[:, o_lora + d_decay + d_aaa:o_hgrn], 256)]
        mu_i = rwkv_mu[i]
        mu_cols = [_pad_cols(mu_i[None, 3 * cr:3 * cr + d_decay], 128),
                   _pad_cols(mu_i[None, 3 * cr + d_decay:3 * cr + d_decay + d_aaa], 128),
                   _pad_cols(mu_i[None, 3 * cr + d_decay + d_aaa:], 256)]
        if vres:
            lora_cols.append(_pad_cols(w_in_vres[i - 1], 128))
            mu_cols.append(_pad_cols(rwkv_mu_vres[i - 1][None], 128))
        w_lora = jnp.concatenate(lora_cols, axis=1).astype(BF16)
        mu_l = jnp.concatenate(mu_cols, axis=1)
        z_lora = _matmul(h_bf, w_lora, out_dtype=F32, tm=tm, tn=w_lora.shape[1])
        r, k, v, ld, kk, a, g = _rwkv_prep(
            z_rkv.reshape(bsz, seq, 3 * cr), z_lora.reshape(bsz, seq, -1),
            mu_i[None, :3 * cr], mu_l,
            rwkv_w0[i][None], _pad_rows(rwkv_w2[i], 128).astype(BF16),
            rwkv_a0[i][None], _pad_rows(rwkv_a2[i], 128).astype(BF16),
            _pad_rows(rwkv_g2[i], 256).astype(BF16),
            rwkv_kk[i][None], rwkv_ka[i][None],
            rwkv_v0[i - 1][None] if vres else None,
            _pad_rows(rwkv_v2[i - 1], 128).astype(BF16) if vres else None,
            v_first, ts=ts)
        if not vres:
            v_first = v
        y_rwkv = _rwkv_scan(r, k, v, ld, kk, a, g, rwkv_rk[i], rwkv_lnx_g[i], rwkv_lnx_b[i],
                            ts=ts, hg=2).reshape(n, cr)

        z_hgrn = _matmul(h_bf, wi[:, o_hgrn:o_gate].astype(BF16), out_dtype=F32, tm=tm, tn=min(1024, 4 * ch))
        y_hgrn = _hgrn_mix(z_hgrn.reshape(bsz, seq, 4 * ch), lower_bounds[i], hgrn_norm_g[i],
                           ts=ts).reshape(n, ch)

        w_g = wi[:, o_gate:o_gate + n_branch * d].reshape(d, n_branch, d).transpose(1, 0, 2).astype(BF16)
        merged = _gated_mm(h_bf, w_g, w_branch[i].astype(BF16), x2s=[y_conv, y_rwkv, y_hgrn],
                           b1=b_gate[i][:, None, :], act="sigmoid", out_dtype=BF16, tm=min(512, n), tn=min(512, d))
        h, h_bf = _proj_res_ln(h, merged, w_out[i].astype(BF16), ln1_g[i][None], ln1_b[i][None],
                               alpha=alpha, tm=min(256, n))

        j = i // 2
        if i % 2 == 0:
            h1 = _gated_mm(h_bf, ffn_wg[j].astype(BF16)[None], ffn_wu[j].astype(BF16)[None],
                           act="silu", out_dtype=BF16, tm=tm, tn=1024)
            terms, coef = [_matmul_ktiled(h1, ffn_wd[j].astype(BF16), tm=tm, tk=1024)], None
        else:
            terms, coef = _moe(h, h_bf, moe_router[j], moe_wg[j].astype(BF16), moe_wu[j].astype(BF16),
                               moe_wd[j].astype(BF16), tm=min(512, n))
        ple = _gated_mm(h_bf, ple_wg[i].astype(BF16)[None], ple_wp[i].astype(BF16)[None],
                        x2s=[p[i].reshape(n, -1).astype(BF16)], b1=ple_bg[i][None, None, :],
                        act="sigmoid", out_dtype=F32, tm=tm, tn=min(1024, d))
        h, h_bf = _add_ln(h, terms, coef, ple, ln2_g[i][None], ln2_b[i][None], alpha=alpha, tm=min(256, n))
    return h.reshape(bsz, seq, d)
```

```python
import functools

import jax
import jax.numpy as jnp
from jax import lax
from jax.experimental import pallas as pl
from jax.experimental.pallas import tpu as pltpu

F32 = jnp.float32
BF16 = jnp.bfloat16
HI = lax.Precision.HIGHEST

V7X_VMEM_LIMIT_BYTES = 48 * 1024 * 1024

LN_EPS = 1e-5
RWKV_LN_EPS = 64e-5
RWKV_HEAD = 64
HGRN_EXPAND = 128
CHUNK = 64
SUB = 16
CONV_HALO = 32
SUBLANES = 8
TOP_K = 2
V7X_MXU_DEPTH = 256

NN_DIMS = (((1,), (0,)), ((), ()))
NT_DIMS = (((1,), (1,)), ((), ()))
TN_DIMS = (((0,), (0,)), ((), ()))


def _params(*semantics):
    return pltpu.CompilerParams(dimension_semantics=semantics,
                                vmem_limit_bytes=V7X_VMEM_LIMIT_BYTES)


def _sigmoid(z):
    return 1.0 / (1.0 + jnp.exp(-z))


def _silu(z):
    return z * _sigmoid(z)


def _act(z, kind):
    return _sigmoid(z) if kind == "sigmoid" else _silu(z)


def _layer_norm(x, g, b, eps):
    mu = jnp.mean(x, axis=-1, keepdims=True)
    xc = x - mu
    var = jnp.mean(xc * xc, axis=-1, keepdims=True)
    return xc * lax.rsqrt(var + eps) * g + b


def _split_bf16(x, terms):
    parts = []
    rest = x
    for _ in range(terms):
        part = rest.astype(BF16).astype(F32)
        parts.append(part)
        rest = rest - part
    return parts


def _mm3(a, b, kind):
    dims = {"nn": NN_DIMS, "nt": NT_DIMS, "tn": TN_DIMS}[kind]
    ax_a = dims[0][0][0]
    ax_b = dims[0][1][0]
    a_hi, a_lo = _split_bf16(a, 2)
    b_hi, b_lo = _split_bf16(b, 2)
    if 3 * a.shape[ax_a] <= V7X_MXU_DEPTH:
        lhs = jnp.concatenate([a_hi, a_lo, a_hi], axis=ax_a).astype(BF16)
        rhs = jnp.concatenate([b_hi, b_hi, b_lo], axis=ax_b).astype(BF16)
        return lax.dot_general(lhs, rhs, dims, preferred_element_type=F32)
    a_hi, a_lo, b_hi, b_lo = (t.astype(BF16) for t in (a_hi, a_lo, b_hi, b_lo))
    return (lax.dot_general(a_hi, b_hi, dims, preferred_element_type=F32)
            + lax.dot_general(a_lo, b_hi, dims, preferred_element_type=F32)
            + lax.dot_general(a_hi, b_lo, dims, preferred_element_type=F32))


def _mask_mm(mask, x):
    return sum(jnp.dot(mask, part.astype(BF16), preferred_element_type=F32) for part in _split_bf16(x, 3))


def _mm_mask(x, mask):
    return sum(jnp.dot(part.astype(BF16), mask, preferred_element_type=F32) for part in _split_bf16(x, 3))


def _chunk_tril_mask(ts):
    row = lax.broadcasted_iota(jnp.int32, (ts, ts), 0)
    col = lax.broadcasted_iota(jnp.int32, (ts, ts), 1)
    shift = CHUNK.bit_length() - 1
    same = lax.shift_right_logical(row, shift) == lax.shift_right_logical(col, shift)
    return jnp.logical_and(same, col <= row).astype(BF16)


def _segment_ones_mask(width, seg):
    row = lax.broadcasted_iota(jnp.int32, (width, width), 0)
    col = lax.broadcasted_iota(jnp.int32, (width, width), 1)
    shift = seg.bit_length() - 1
    return (lax.shift_right_logical(row, shift) == lax.shift_right_logical(col, shift)).astype(BF16)


def _mm_kernel(x_ref, w_ref, o_ref):
    o_ref[...] = jnp.dot(x_ref[...], w_ref[...], preferred_element_type=F32).astype(o_ref.dtype)


def _matmul(x, w, *, out_dtype, tm, tn):
    m, k = x.shape
    n = w.shape[1]
    return pl.pallas_call(
        _mm_kernel,
        grid=(m // tm, n // tn),
        in_specs=[pl.BlockSpec((tm, k), lambda i, j: (i, 0)),
                  pl.BlockSpec((k, tn), lambda i, j: (0, j))],
        out_specs=pl.BlockSpec((tm, tn), lambda i, j: (i, j)),
        out_shape=jax.ShapeDtypeStruct((m, n), out_dtype),
        compiler_params=_params("parallel", "arbitrary"),
        name="matmul",
    )(x, w)


def _gated_mm_body(x1_ref, w1_ref, b1_ref, x2_refs, w2_ref, o_ref, *, nb, act):
    a = x1_ref[...]
    acc = None
    for n in range(nb):
        z = jnp.dot(a, w1_ref[n], preferred_element_type=F32)
        if b1_ref is not None:
            z = z + b1_ref[n]
        lhs2 = a if x2_refs is None else x2_refs[n][...]
        lin = jnp.dot(lhs2, w2_ref[n], preferred_element_type=F32)
        t = _act(z, act) * lin
        acc = t if acc is None else acc + t
    o_ref[...] = acc.astype(o_ref.dtype)


def _gated_mm_kernel(*refs, nb, act, has_bias, n_x2):
    refs = list(refs)
    x1_ref, w1_ref = refs[0], refs[1]
    pos = 2
    b1_ref = None
    if has_bias:
        b1_ref = refs[pos]
        pos += 1
    x2_refs = refs[pos:pos + n_x2] if n_x2 else None
    pos += n_x2
    w2_ref, o_ref = refs[pos], refs[pos + 1]
    _gated_mm_body(x1_ref, w1_ref, b1_ref, x2_refs, w2_ref, o_ref, nb=nb, act=act)


def _gated_mm(x1, w1, w2, *, x2s=None, b1=None, act, out_dtype, tm, tn):
    m, k1 = x1.shape
    nb, _, n = w1.shape
    k2 = w2.shape[1]
    in_specs = [pl.BlockSpec((tm, k1), lambda i, j: (i, 0)),
                pl.BlockSpec((nb, k1, tn), lambda i, j: (0, 0, j))]
    args = [x1, w1]
    if b1 is not None:
        in_specs.append(pl.BlockSpec((nb, 1, tn), lambda i, j: (0, 0, j)))
        args.append(b1)
    n_x2 = 0
    if x2s is not None:
        n_x2 = len(x2s)
        for x2 in x2s:
            in_specs.append(pl.BlockSpec((tm, k2), lambda i, j: (i, 0)))
            args.append(x2)
    in_specs.append(pl.BlockSpec((nb, k2, tn), lambda i, j: (0, 0, j)))
    args.append(w2)
    return pl.pallas_call(
        functools.partial(_gated_mm_kernel, nb=nb, act=act, has_bias=b1 is not None, n_x2=n_x2),
        grid=(m // tm, n // tn),
        in_specs=in_specs,
        out_specs=pl.BlockSpec((tm, tn), lambda i, j: (i, j)),
        out_shape=jax.ShapeDtypeStruct((m, n), out_dtype),
        compiler_params=_params("parallel", "arbitrary"),
        name="gated_matmul",
    )(*args)


def _grouped_gated_kernel(te_ref, nt_ref, x_ref, w1_ref, w2_ref, o_ref, *, act):
    i = pl.program_id(0)

    @pl.when(i < nt_ref[0])
    def _():
        a = x_ref[...]
        z = jnp.dot(a, w1_ref[0], preferred_element_type=F32)
        lin = jnp.dot(a, w2_ref[0], preferred_element_type=F32)
        o_ref[...] = (_act(z, act) * lin).astype(o_ref.dtype)

    @pl.when(i >= nt_ref[0])
    def _():
        o_ref[...] = jnp.zeros_like(o_ref)


def _grouped_gated_mm(x, w1, w2, tile_expert, n_tiles, *, act, out_dtype, tm, tn):
    m, k = x.shape
    n = w1.shape[2]
    grid_spec = pltpu.PrefetchScalarGridSpec(
        num_scalar_prefetch=2,
        grid=(m // tm, n // tn),
        in_specs=[pl.BlockSpec((tm, k), lambda i, j, te, nt: (i, 0)),
                  pl.BlockSpec((1, k, tn), lambda i, j, te, nt: (te[i], 0, j)),
                  pl.BlockSpec((1, k, tn), lambda i, j, te, nt: (te[i], 0, j))],
        out_specs=pl.BlockSpec((tm, tn), lambda i, j, te, nt: (i, j)),
    )
    return pl.pallas_call(
        functools.partial(_grouped_gated_kernel, act=act),
        grid_spec=grid_spec,
        out_shape=jax.ShapeDtypeStruct((m, n), out_dtype),
        compiler_params=_params("arbitrary", "arbitrary"),
        name="grouped_gated_matmul",
    )(tile_expert, n_tiles, x, w1, w2)


def _mm_acc_kernel(x_ref, w_ref, o_ref):
    k = pl.program_id(1)
    part = jnp.dot(x_ref[...], w_ref[...], preferred_element_type=F32)

    @pl.when(k == 0)
    def _():
        o_ref[...] = part

    @pl.when(k > 0)
    def _():
        o_ref[...] += part


def _matmul_ktiled(x, w, *, tm, tk):
    m, kdim = x.shape
    n = w.shape[1]
    return pl.pallas_call(
        _mm_acc_kernel,
        grid=(m // tm, kdim // tk),
        in_specs=[pl.BlockSpec((tm, tk), lambda i, k: (i, k)),
                  pl.BlockSpec((tk, n), lambda i, k: (k, 0))],
        out_specs=pl.BlockSpec((tm, n), lambda i, k: (i, 0)),
        out_shape=jax.ShapeDtypeStruct((m, n), F32),
        compiler_params=_params("parallel", "arbitrary"),
        name="matmul_ktiled",
    )(x, w)


def _grouped_mm_acc_kernel(te_ref, nt_ref, x_ref, w_ref, o_ref):
    i = pl.program_id(0)
    k = pl.program_id(1)

    @pl.when(i < nt_ref[0])
    def _():
        part = jnp.dot(x_ref[...], w_ref[0], preferred_element_type=F32)

        @pl.when(k == 0)
        def _():
            o_ref[...] = part

        @pl.when(k > 0)
        def _():
            o_ref[...] += part

    @pl.when(jnp.logical_and(i >= nt_ref[0], k == 0))
    def _():
        o_ref[...] = jnp.zeros_like(o_ref)


def _grouped_matmul_ktiled(x, w, tile_expert, n_tiles, *, tm, tk):
    m, kdim = x.shape
    n = w.shape[2]
    grid_spec = pltpu.PrefetchScalarGridSpec(
        num_scalar_prefetch=2,
        grid=(m // tm, kdim // tk),
        in_specs=[pl.BlockSpec((tm, tk), lambda i, k, te, nt: (i, k)),
                  pl.BlockSpec((1, tk, n), lambda i, k, te, nt: (te[i], k, 0))],
        out_specs=pl.BlockSpec((tm, n), lambda i, k, te, nt: (i, 0)),
    )
    return pl.pallas_call(
        _grouped_mm_acc_kernel,
        grid_spec=grid_spec,
        out_shape=jax.ShapeDtypeStruct((m, n), F32),
        compiler_params=_params("arbitrary", "arbitrary"),
        name="grouped_matmul_ktiled",
    )(tile_expert, n_tiles, x, w)


def _proj_res_ln_kernel(h_ref, m_ref, w_ref, g_ref, b_ref, o_ref, obf_ref, *, alpha):
    y = alpha * h_ref[...] + jnp.dot(m_ref[...], w_ref[...], preferred_element_type=F32)
    out = _layer_norm(y, g_ref[...], b_ref[...], LN_EPS)
    o_ref[...] = out
    obf_ref[...] = out.astype(BF16)


def _proj_res_ln(h, m_in, w, g, b, *, alpha, tm):
    n, d = h.shape
    k = m_in.shape[1]
    return pl.pallas_call(
        functools.partial(_proj_res_ln_kernel, alpha=alpha),
        grid=(n // tm,),
        in_specs=[pl.BlockSpec((tm, d), lambda i: (i, 0)),
                  pl.BlockSpec((tm, k), lambda i: (i, 0)),
                  pl.BlockSpec((k, d), lambda i: (0, 0)),
                  pl.BlockSpec((1, d), lambda i: (0, 0)),
                  pl.BlockSpec((1, d), lambda i: (0, 0))],
        out_specs=[pl.BlockSpec((tm, d), lambda i: (i, 0)),
                   pl.BlockSpec((tm, d), lambda i: (i, 0))],
        out_shape=[jax.ShapeDtypeStruct((n, d), F32), jax.ShapeDtypeStruct((n, d), BF16)],
        compiler_params=_params("parallel"),
        name="proj_res_ln",
    )(h, m_in, w, g, b)


def _add_ln_kernel(*refs, alpha, n_terms, weighted):
    h_ref = refs[0]
    a_refs = refs[1:1 + n_terms]
    pos = 1 + n_terms
    c_ref = None
    if weighted:
        c_ref = refs[pos]
        pos += 1
    ple_ref, g_ref, b_ref, o_ref, obf_ref = refs[pos:pos + 5]
    y = alpha * h_ref[...] + ple_ref[...]
    for j in range(n_terms):
        t = a_refs[j][...]
        if weighted:
            t = t * c_ref[:, j:j + 1]
        y = y + t
    out = _layer_norm(y, g_ref[...], b_ref[...], LN_EPS)
    o_ref[...] = out
    obf_ref[...] = out.astype(BF16)


def _add_ln(h, terms, coef, ple, g, b, *, alpha, tm):
    n, d = h.shape
    row = pl.BlockSpec((tm, d), lambda i: (i, 0))
    vec = pl.BlockSpec((1, d), lambda i: (0, 0))
    in_specs = [row] + [row] * len(terms)
    args = [h] + list(terms)
    if coef is not None:
        in_specs.append(pl.BlockSpec((tm, coef.shape[1]), lambda i: (i, 0)))
        args.append(coef)
    in_specs += [row, vec, vec]
    args += [ple, g, b]
    return pl.pallas_call(
        functools.partial(_add_ln_kernel, alpha=alpha, n_terms=len(terms), weighted=coef is not None),
        grid=(n // tm,),
        in_specs=in_specs,
        out_specs=[row, row],
        out_shape=[jax.ShapeDtypeStruct((n, d), F32), jax.ShapeDtypeStruct((n, d), BF16)],
        compiler_params=_params("parallel"),
        name="add_ln",
    )(*args)


def _conv_kernel(u_ref, dw_ref, dwb_ref, g_ref, b_ref, o_ref, buf, shifted, *, ts, rb, width):
    s = pl.program_id(1)

    @pl.when(s == 0)
    def _():
        buf[0:CONV_HALO, :] = jnp.zeros((CONV_HALO, buf.shape[1]), F32)

    buf[CONV_HALO:CONV_HALO + ts, :] = u_ref[0]
    lead = CONV_HALO - (width - 1)
    span = shifted.shape[1]
    for r in range(1, SUBLANES):
        shifted[r - 1] = buf[r:r + span, :]
    for r0 in range(0, ts, rb):
        acc = jnp.broadcast_to(dwb_ref[...], (rb, buf.shape[1]))
        for j in range(width):
            m, r = divmod(lead + j, SUBLANES)
            lo = r0 + SUBLANES * m
            rows = buf[lo:lo + rb, :] if r == 0 else shifted[r - 1, lo:lo + rb, :]
            acc = acc + rows * dw_ref[SUBLANES * j:SUBLANES * (j + 1), :]
        y = _silu(_layer_norm(acc, g_ref[...], b_ref[...], LN_EPS))
        o_ref[0, r0:r0 + rb, :] = y.astype(o_ref.dtype)
    buf[0:CONV_HALO, :] = buf[ts:ts + CONV_HALO, :]


def _conv_module(u, dw, dw_b, ln_g, ln_b, *, ts, rb):
    bsz, seq, c = u.shape
    width = dw.shape[0]
    assert rb == SUBLANES
    dw_rep = jnp.repeat(dw.astype(F32), SUBLANES, axis=0)
    vec = pl.BlockSpec((1, c), lambda b, s: (0, 0))
    return pl.pallas_call(
        functools.partial(_conv_kernel, ts=ts, rb=rb, width=width),
        grid=(bsz, seq // ts),
        in_specs=[pl.BlockSpec((1, ts, c), lambda b, s: (b, s, 0)),
                  pl.BlockSpec((width * SUBLANES, c), lambda b, s: (0, 0)),
                  vec, vec, vec],
        out_specs=pl.BlockSpec((1, ts, c), lambda b, s: (b, s, 0)),
        out_shape=jax.ShapeDtypeStruct((bsz, seq, c), BF16),
        scratch_shapes=[pltpu.VMEM((CONV_HALO + ts, c), F32),
                        pltpu.VMEM((SUBLANES - 1, CONV_HALO + ts - SUBLANES, c), F32)],
        compiler_params=_params("parallel", "arbitrary"),
        name="conv_module",
    )(u, dw_rep, dw_b.reshape(1, c), ln_g.reshape(1, c), ln_b.reshape(1, c))


def _hgrn_kernel(zq_ref, zf_ref, zi_ref, zg_ref, lb_ref, ng_ref, o_ref, state_t, k_buf, b_buf, *, ts):
    s = pl.program_id(2)
    nc = ts // CHUNK
    nsub = CHUNK // SUB

    @pl.when(s == 0)
    def _():
        state_t[...] = jnp.zeros_like(state_t)

    q = _silu(zq_ref[0]) * (HGRN_EXPAND ** -0.5)
    k = (1.0 - lb_ref[...]) * _sigmoid(-zf_ref[0])
    v = zi_ref[0]
    b = _mask_mm(_chunk_tril_mask(ts), jnp.log(1.0 - k))
    k_buf[...] = k
    b_buf[...] = b

    row = lax.broadcasted_iota(jnp.int32, (CHUNK, CHUNK), 0)
    col = lax.broadcasted_iota(jnp.int32, (CHUNK, CHUNK), 1)
    row_sub = lax.shift_right_logical(row, SUB.bit_length() - 1)
    col_sub = lax.shift_right_logical(col, SUB.bit_length() - 1)
    earlier_sub = col_sub < row_sub
    same_sub_causal = jnp.logical_and(col_sub == row_sub, col <= row)
    strip_col = lax.broadcasted_iota(jnp.int32, (SUB, CHUNK), 1)

    o_intra, upd, decay, q_dec = [], [], [], []
    for c in range(nc):
        base = c * CHUNK
        q_c, k_c, b_c, v_c = (t[base:base + CHUNK] for t in (q, k, b, v))
        off_strips, diag_strips = [], []
        for i in range(nsub):
            lo = base + i * SUB
            q_i, b_i = q[lo:lo + SUB], b[lo:lo + SUB]
            if i == 0:
                off_strips.append(jnp.zeros((SUB, CHUNK), F32))
            else:
                beta = b_buf[lo - 1:lo, :]
                q_hat = q_i * jnp.exp(b_i - beta)
                k_hat = k_c * jnp.exp(jnp.minimum(beta - b_c, 0.0))
                off_strips.append(_mm3(q_hat, k_hat, "nt"))
            strip = jnp.zeros((SUB, CHUNK), F32)
            for j in range(SUB):
                sj = i * SUB + j
                k_row = k_buf[lo + j:lo + j + 1, :]
                b_row = b_buf[lo + j:lo + j + 1, :]
                e = jnp.exp(jnp.minimum(b_i - b_row, 0.0))
                colv = jnp.sum(q_i * (k_row * e), axis=-1, keepdims=True)
                strip = jnp.where(strip_col == sj, colv, strip)
            diag_strips.append(strip)
        att = (jnp.where(earlier_sub, jnp.concatenate(off_strips, axis=0), 0.0)
               + jnp.where(same_sub_causal, jnp.concatenate(diag_strips, axis=0), 0.0))
        o_intra.append(_mm3(att, v_c, "nn"))
        b_last = b_buf[base + CHUNK - 1:base + CHUNK, :]
        upd.append(_mm3(v_c, k_c * jnp.exp(b_last - b_c), "tn"))
        decay.append(jnp.exp(b_last))
        q_dec.append(q_c * jnp.exp(b_c))

    st = state_t[...]
    outs = []
    for c in range(nc):
        outs.append(o_intra[c] + _mm3(q_dec[c], st, "nt"))
        st = st * decay[c] + upd[c]
    state_t[...] = st
    o = jnp.concatenate(outs, axis=0)
    o = o * lax.rsqrt(jnp.mean(o * o, axis=-1, keepdims=True) + LN_EPS) * ng_ref[...]
    o_ref[0] = (o * _silu(zg_ref[0])).astype(o_ref.dtype)


def _hgrn_mix(z, lb, norm_g, *, ts):
    bsz, seq, c4 = z.shape
    dh = HGRN_EXPAND
    heads = c4 // (4 * dh)

    def zspec(group):
        return pl.BlockSpec((1, ts, dh), lambda b, h, s: (b, s, group * heads + h))

    return pl.pallas_call(
        functools.partial(_hgrn_kernel, ts=ts),
        grid=(bsz, heads, seq // ts),
        in_specs=[zspec(0), zspec(1), zspec(2), zspec(3),
                  pl.BlockSpec((1, dh), lambda b, h, s: (0, h)),
                  pl.BlockSpec((1, dh), lambda b, h, s: (0, 0))],
        out_specs=pl.BlockSpec((1, ts, dh), lambda b, h, s: (b, s, h)),
        out_shape=jax.ShapeDtypeStruct((bsz, seq, heads * dh), BF16),
        scratch_shapes=[pltpu.VMEM((dh, dh), F32),
                        pltpu.VMEM((ts, dh), F32),
                        pltpu.VMEM((ts, dh), F32)],
        compiler_params=_params("parallel", "parallel", "arbitrary"),
        name="hgrn_mix",
    )(z, z, z, z, lb.reshape(1, heads * dh), norm_g.reshape(1, dh))


def _token_shift(z, prev_row, mu):
    rows = lax.broadcasted_iota(jnp.int32, z.shape, 0)
    shifted = jnp.where(rows == 0, prev_row, pltpu.roll(z, 1, axis=0))
    return z + (shifted - z) * mu


def _rwkv_prep_kernel(*refs, vres, c):
    (zrkv_ref, zl_ref, mu_rkv_ref, mu_l_ref, w0_ref, w2_ref, a0_ref, a2_ref, g2_ref,
     kk_ref, ka_ref) = refs[:11]
    pos = 11
    if vres:
        v0_ref, v2_ref, vfirst_ref = refs[pos:pos + 3]
        pos += 3
    r_o, k_o, v_o, ld_o, kk_o, a_o, g_o, prev_rkv, prev_l = refs[pos:pos + 9]
    s = pl.program_id(1)

    @pl.when(s == 0)
    def _():
        prev_rkv[...] = jnp.zeros_like(prev_rkv)
        prev_l[...] = jnp.zeros_like(prev_l)

    z = zrkv_ref[0]
    zl = zl_ref[0]
    ts = z.shape[0]
    z_mix = _token_shift(z, prev_rkv[...], mu_rkv_ref[...])
    zl_mix = _token_shift(zl, prev_l[...], mu_l_ref[...])
    prev_rkv[...] = z[ts - 1:ts, :]
    prev_l[...] = zl[ts - 1:ts, :]

    zr, zk, zv = z_mix[:, 0:c], z_mix[:, c:2 * c], z_mix[:, 2 * c:3 * c]
    zw, za, zg = zl_mix[:, 0:128], zl_mix[:, 128:256], zl_mix[:, 256:512]

    def lora(act_in, w_ref):
        return jnp.dot(act_in.astype(BF16), w_ref[...], preferred_element_type=F32)

    x = -(w0_ref[...] + lora(jnp.tanh(zw), w2_ref))
    softplus = jnp.maximum(x, 0.0) + jnp.log(1.0 + jnp.exp(-jnp.abs(x)))
    w = -softplus - 0.5
    a = _sigmoid(a0_ref[...] + lora(za, a2_ref))
    g = lora(_sigmoid(zg), g2_ref)
    v = zv
    if vres:
        zm = zl_mix[:, 512:640]
        v = zv + (vfirst_ref[0] - zv) * _sigmoid(v0_ref[...] + lora(zm, v2_ref))
    r_o[0] = zr
    k_o[0] = zk * (1.0 + (a - 1.0) * ka_ref[...])
    v_o[0] = v
    ld_o[0] = -jnp.exp(w)
    kk_o[0] = zk * kk_ref[...]
    a_o[0] = a
    g_o[0] = g


def _rwkv_prep(z_rkv, z_lora, mu_rkv, mu_l, w0, w2, a0, a2, g2, k_k, k_a, v0, v2, v_first, *, ts):
    bsz, seq, c3 = z_rkv.shape
    c = c3 // 3
    lw = z_lora.shape[2]
    vres = v0 is not None

    def full(arr):
        return pl.BlockSpec(arr.shape, lambda b, s: (0,) * arr.ndim)

    tile = pl.BlockSpec((1, ts, c), lambda b, s: (b, s, 0))
    args = [z_rkv, z_lora, mu_rkv, mu_l, w0, w2, a0, a2, g2, k_k, k_a]
    in_specs = [pl.BlockSpec((1, ts, c3), lambda b, s: (b, s, 0)),
                pl.BlockSpec((1, ts, lw), lambda b, s: (b, s, 0))] + [full(a) for a in args[2:]]
    if vres:
        args += [v0, v2, v_first]
        in_specs += [full(v0), full(v2), tile]
    return pl.pallas_call(
        functools.partial(_rwkv_prep_kernel, vres=vres, c=c),
        grid=(bsz, seq // ts),
        in_specs=in_specs,
        out_specs=[tile] * 7,
        out_shape=[jax.ShapeDtypeStruct((bsz, seq, c), F32)] * 7,
        scratch_shapes=[pltpu.VMEM((1, c3), F32), pltpu.VMEM((1, lw), F32)],
        compiler_params=_params("parallel", "arbitrary"),
        name="rwkv_prep",
    )(*args)


def _rwkv_scan_kernel(r_ref, k_ref, v_ref, ld_ref, kk_ref, a_ref, g_ref, rk_ref, lng_ref, lnb_ref,
                      o_ref, state, *, ts, hg):
    s = pl.program_id(2)
    n = RWKV_HEAD
    nc = ts // CHUNK
    width = hg * n

    @pl.when(s == 0)
    def _():
        state[...] = jnp.zeros_like(state)

    r, k, v, ld, kk, a = (ref[0] for ref in (r_ref, k_ref, v_ref, ld_ref, kk_ref, a_ref))
    head_ones = _segment_ones_mask(width, n)
    lp = _mask_mm(_chunk_tril_mask(ts), ld)
    norm = jnp.sqrt(_mm_mask(kk * kk, head_ones))
    kkn = kk / jnp.maximum(norm, 1e-12)
    bvec = kkn * a
    inv_p = jnp.exp(-lp)
    a_t = -kkn * jnp.exp(lp - ld)
    r_t = r * jnp.exp(lp)
    b_t = bvec * inv_p
    k_t = k * inv_p

    row = lax.broadcasted_iota(jnp.int32, (CHUNK, CHUNK), 0)
    col = lax.broadcasted_iota(jnp.int32, (CHUNK, CHUNK), 1)
    incl = row >= col
    strict = row > col
    eye = (row == col).astype(F32)

    cells = [(c, hh) for c in range(nc) for hh in range(hg)]

    def blk(x, c, hh):
        return x[c * CHUNK:(c + 1) * CHUNK, hh * n:(hh + 1) * n]

    p_last = [jnp.exp(lp[(c + 1) * CHUNK - 1:(c + 1) * CHUNK, :]) for c in range(nc)]
    to_end = jnp.concatenate(
        [jnp.exp(lp[(c + 1) * CHUNK - 1:(c + 1) * CHUNK, :] - lp[c * CHUNK:(c + 1) * CHUNK, :]) for c in range(nc)],
        axis=0)
    b_h = bvec * to_end
    k_h = k * to_end

    a_c = {cell: blk(a_t, *cell) for cell in cells}
    v_c = {cell: blk(v, *cell) for cell in cells}
    gram = {cell: _mm3(jnp.concatenate([a_c[cell], blk(r_t, *cell)], axis=0),
                       jnp.concatenate([blk(b_t, *cell), blk(k_t, *cell)], axis=0), "nt") for cell in cells}
    t_ab = {cell: jnp.where(strict, gram[cell][0:CHUNK, 0:CHUNK], 0.0) for cell in cells}
    t_ak = {cell: jnp.where(strict, gram[cell][0:CHUNK, CHUNK:], 0.0) for cell in cells}
    t_rb = {cell: jnp.where(incl, gram[cell][CHUNK:, 0:CHUNK], 0.0) for cell in cells}
    t_rk = {cell: jnp.where(incl, gram[cell][CHUNK:, CHUNK:], 0.0) for cell in cells}

    def square(p):
        return {cell: _mm3(p[cell], p[cell], "nn") for cell in cells}

    def pair(p_lo, p_hi):
        return {cell: eye + p_lo[cell] + _mm3(eye + p_lo[cell], p_hi[cell], "nn") for cell in cells}

    p2 = square(t_ab)
    f01 = pair(t_ab, p2)
    p4 = square(p2)
    p8 = square(p4)
    f23 = pair(p4, p8)
    f03 = {cell: _mm3(f01[cell], f23[cell], "nn") for cell in cells}
    p16 = square(p8)
    p32 = square(p16)
    f45 = pair(p16, p32)
    inv = {cell: _mm3(f03[cell], f45[cell], "nn") for cell in cells}

    tv = {cell: _mm3(jnp.concatenate([t_ak[cell], t_rk[cell]], axis=0), v_c[cell], "nn") for cell in cells}
    wu = {cell: _mm3(inv[cell], jnp.concatenate([a_c[cell], tv[cell][0:CHUNK]], axis=1), "nn") for cell in cells}
    qy = {cell: _mm3(t_rb[cell], wu[cell], "nn") for cell in cells}
    gh = {cell: _mm3(blk(b_h, *cell), wu[cell], "tn") for cell in cells}
    kv = {cell: _mm3(blk(k_h, *cell), v_c[cell], "tn") for cell in cells}
    q_g, y0, h_t = {}, {}, {}
    for cell in cells:
        c, hh = cell
        q_eff = blk(r_t, *cell) + qy[cell][:, 0:n]
        g_t = eye * p_last[c][:, hh * n:(hh + 1) * n] + gh[cell][:, 0:n]
        q_g[cell] = jnp.concatenate([q_eff, g_t], axis=0)
        y0[cell] = qy[cell][:, n:] + tv[cell][CHUNK:]
        h_t[cell] = gh[cell][:, n:] + kv[cell]

    z = [state[hh] for hh in range(hg)]
    y_rows = []
    for c in range(nc):
        y_heads = []
        for hh in range(hg):
            res = _mm3(q_g[(c, hh)], z[hh], "nn")
            y_heads.append(res[0:CHUNK] + y0[(c, hh)])
            z[hh] = res[CHUNK:] + h_t[(c, hh)]
        y_rows.append(jnp.concatenate(y_heads, axis=1))
    for hh in range(hg):
        state[hh] = z[hh]
    y = jnp.concatenate(y_rows, axis=0)

    inv_n = 1.0 / n
    mu = _mm_mask(y, head_ones) * inv_n
    yc = y - mu
    var = _mm_mask(yc * yc, head_ones) * inv_n
    yn = yc * lax.rsqrt(var + RWKV_LN_EPS) * lng_ref[...] + lnb_ref[...]
    bonus = _mm_mask(r * k * rk_ref[...], head_ones) * v
    o_ref[0] = ((yn + bonus) * g_ref[0]).astype(o_ref.dtype)


def _rwkv_scan(r, k, v, ld, kk, a, g, r_k, lnx_g, lnx_b, *, ts, hg):
    bsz, seq, c = r.shape
    wblk = hg * RWKV_HEAD
    tile = pl.BlockSpec((1, ts, wblk), lambda b, h, s: (b, s, h))
    vec = pl.BlockSpec((1, wblk), lambda b, h, s: (0, h))
    return pl.pallas_call(
        functools.partial(_rwkv_scan_kernel, ts=ts, hg=hg),
        grid=(bsz, c // wblk, seq // ts),
        in_specs=[tile] * 7 + [vec] * 3,
        out_specs=tile,
        out_shape=jax.ShapeDtypeStruct((bsz, seq, c), BF16),
        scratch_shapes=[pltpu.VMEM((hg, RWKV_HEAD, RWKV_HEAD), F32)],
        compiler_params=_params("parallel", "parallel", "arbitrary"),
        name="rwkv_scan",
    )(r, k, v, ld, kk, a, g, r_k.reshape(1, c), lnx_g.reshape(1, c), lnx_b.reshape(1, c))


def _router_kernel(h_ref, w_ref, idx_ref, wt_ref, *, n_experts):
    logits = jnp.dot(h_ref[...], w_ref[...], precision=HI, preferred_element_type=F32)
    lane = lax.broadcasted_iota(jnp.int32, logits.shape, 1)
    neg = jnp.float32(-jnp.inf)
    big = jnp.int32(logits.shape[1])
    x = jnp.where(lane < n_experts, logits, neg)
    m1 = jnp.max(x, axis=-1, keepdims=True)
    i1 = jnp.min(jnp.where(x == m1, lane, big), axis=-1, keepdims=True)
    x2 = jnp.where(lane == i1, neg, x)
    m2 = jnp.max(x2, axis=-1, keepdims=True)
    i2 = jnp.min(jnp.where(x2 == m2, lane, big), axis=-1, keepdims=True)
    e2 = jnp.exp(m2 - m1)
    w1 = 1.0 / (1.0 + e2)
    w2 = e2 / (1.0 + e2)
    idx_ref[...] = jnp.where(lane == 0, i1, jnp.where(lane == 1, i2, 0))
    wt_ref[...] = jnp.where(lane == 0, w1, jnp.where(lane == 1, w2, 0.0))


def _router(h, w_router, *, tm):
    n, d = h.shape
    n_experts = w_router.shape[1]
    w_pad = jnp.zeros((d, 128), F32).at[:, :n_experts].set(w_router)
    out = pl.BlockSpec((tm, 128), lambda i: (i, 0))
    return pl.pallas_call(
        functools.partial(_router_kernel, n_experts=n_experts),
        grid=(n // tm,),
        in_specs=[pl.BlockSpec((tm, d), lambda i: (i, 0)),
                  pl.BlockSpec((d, 128), lambda i: (0, 0))],
        out_specs=[out, out],
        out_shape=[jax.ShapeDtypeStruct((n, 128), jnp.int32), jax.ShapeDtypeStruct((n, 128), F32)],
        compiler_params=_params("parallel"),
        name="moe_router",
    )(h, w_pad)


def _moe(h, h_bf, w_router, wg, wu, wd, *, tm):
    n, d = h.shape
    n_experts = wg.shape[0]
    idx_pad, wt_pad = _router(h, w_router, tm=512)
    e_flat = idx_pad[:, :TOP_K].reshape(-1)
    onehot = (e_flat[:, None] == jnp.arange(n_experts, dtype=jnp.int32)[None, :]).astype(jnp.int32)
    csum = jnp.cumsum(onehot, axis=0)
    rank = jnp.sum(csum * onehot, axis=1) - 1
    counts = csum[-1]
    padded = ((counts + tm - 1) // tm) * tm
    ends = jnp.cumsum(padded)
    starts = ends - padded
    pos = starts[e_flat] + rank
    n_rows = TOP_K * n + n_experts * tm
    src = jnp.zeros((n_rows,), jnp.int32).at[pos].set(jnp.arange(TOP_K * n, dtype=jnp.int32) // TOP_K)
    n_tiles = (ends[-1] // tm).astype(jnp.int32).reshape(1)
    tile_start = jnp.arange(n_rows // tm, dtype=jnp.int32) * tm
    tile_expert = jnp.minimum(jnp.searchsorted(ends, tile_start, side="right"), n_experts - 1).astype(jnp.int32)
    x_sorted = h_bf.at[src].get(mode="promise_in_bounds")
    h1 = _grouped_gated_mm(x_sorted, wg, wu, tile_expert, n_tiles, act="silu", out_dtype=BF16, tm=tm, tn=1024)
    y_sorted = _grouped_matmul_ktiled(h1, wd, tile_expert, n_tiles, tm=tm, tk=1024)
    pos2 = pos.reshape(n, TOP_K)
    y0 = y_sorted.at[pos2[:, 0]].get(mode="promise_in_bounds")
    y1 = y_sorted.at[pos2[:, 1]].get(mode="promise_in_bounds")
    return [y0, y1], wt_pad


def _pad_cols(w, width):
    return jnp.pad(w, ((0, 0), (0, width - w.shape[1])))


def _pad_rows(w, height):
    return jnp.pad(w, ((0, height - w.shape[0]), (0, 0)))


def kernel(x, p, w_in, w_in_vres, conv_dw, conv_dw_b, conv_ln_g, conv_ln_b, rwkv_mu, rwkv_mu_vres, rwkv_w0, rwkv_w2, rwkv_a0, rwkv_a2, rwkv_v0, rwkv_v2, rwkv_g2, rwkv_kk, rwkv_ka, rwkv_rk, rwkv_lnx_g, rwkv_lnx_b, hgrn_lb, hgrn_norm_g, w_branch, b_gate, w_out, ln1_g, ln1_b, ffn_wg, ffn_wu, ffn_wd, moe_router, moe_wg, moe_wu, moe_wd, ple_wp, ple_wg, ple_bg, ln2_g, ln2_b):
    bsz, seq, d = x.shape
    depth = w_in.shape[0]
    n = bsz * seq
    cc = conv_dw.shape[2]
    cr = rwkv_w0.shape[1]
    ch = hgrn_lb.shape[1]
    d_decay, d_aaa, d_gate, d_mv = rwkv_w2.shape[1], rwkv_a2.shape[1], rwkv_g2.shape[1], rwkv_v2.shape[1]
    n_branch = w_branch.shape[1]
    alpha = float((2 * depth) ** 0.25)
    tm = min(1024, n)
    ts = min(256, seq)

    lb_soft = jax.nn.softmax(hgrn_lb.astype(F32), axis=0)
    lower_bounds = jnp.cumsum(lb_soft, axis=0) - lb_soft[0]

    h = x.reshape(n, d)
    h_bf = h.astype(BF16)
    v_first = None
    for i in range(depth):
        vres = i > 0
        wi = w_in[i]
        o_rwkv = 2 * cc
        o_lora = o_rwkv + 3 * cr
        o_hgrn = o_lora + d_decay + d_aaa + d_gate
        o_gate = o_hgrn + 4 * ch

        w_val = wi[:, 0:cc].astype(BF16)[None]
        w_glu = wi[:, cc:2 * cc].astype(BF16)[None]
        u = _gated_mm(h_bf, w_glu, w_val, act="sigmoid", out_dtype=F32, tm=tm, tn=min(1024, cc))
        y_conv = _conv_module(u.reshape(bsz, seq, cc), conv_dw[i], conv_dw_b[i], conv_ln_g[i], conv_ln_b[i],
                              ts=ts, rb=SUBLANES).reshape(n, cc)

        z_rkv = _matmul(h_bf, wi[:, o_rwkv:o_lora].astype(BF16), out_dtype=F32, tm=tm, tn=min(1024, cr))
        lora_cols = [_pad_cols(wi[:, o_lora:o_lora + d_decay], 128),
                     _pad_cols(wi[:, o_lora + d_decay:o_lora + d_decay + d_aaa], 128),
                     _pad_cols(wi[:, o_lora + d_decay + d_aaa:o_hgrn], 256)]
        mu_i = rwkv_mu[i]
        mu_cols = [_pad_cols(mu_i[None, 3 * cr:3 * cr + d_decay], 128),
                   _pad_cols(mu_i[None, 3 * cr + d_decay:3 * cr + d_decay + d_aaa], 128),
                   _pad_cols(mu_i[None, 3 * cr + d_decay + d_aaa:], 256)]
        if vres:
            lora_cols.append(_pad_cols(w_in_vres[i - 1], 128))
            mu_cols.append(_pad_cols(rwkv_mu_vres[i - 1][None], 128))
        w_lora = jnp.concatenate(lora_cols, axis=1).astype(BF16)
        mu_l = jnp.concatenate(mu_cols, axis=1)
        z_lora = _matmul(h_bf, w_lora, out_dtype=F32, tm=tm, tn=w_lora.shape[1])
        r, k, v, ld, kk, a, g = _rwkv_prep(
            z_rkv.reshape(bsz, seq, 3 * cr), z_lora.reshape(bsz, seq, -1),
            mu_i[None, :3 * cr], mu_l,
            rwkv_w0[i][None], _pad_rows(rwkv_w2[i], 128).astype(BF16),
            rwkv_a0[i][None], _pad_rows(rwkv_a2[i], 128).astype(BF16),
            _pad_rows(rwkv_g2[i], 256).astype(BF16),
            rwkv_kk[i][None], rwkv_ka[i][None],
            rwkv_v0[i - 1][None] if vres else None,
            _pad_rows(rwkv_v2[i - 1], 128).astype(BF16) if vres else None,
            v_first, ts=ts)
        if not vres:
            v_first = v
        y_rwkv = _rwkv_scan(r, k, v, ld, kk, a, g, rwkv_rk[i], rwkv_lnx_g[i], rwkv_lnx_b[i],
                            ts=ts, hg=2).reshape(n, cr)

        z_hgrn = _matmul(h_bf, wi[:, o_hgrn:o_gate].astype(BF16), out_dtype=F32, tm=tm, tn=min(1024, 4 * ch))
        y_hgrn = _hgrn_mix(z_hgrn.reshape(bsz, seq, 4 * ch), lower_bounds[i], hgrn_norm_g[i],
                           ts=ts).reshape(n, ch)

        w_g = wi[:, o_gate:o_gate + n_branch * d].reshape(d, n_branch, d).transpose(1, 0, 2).astype(BF16)
        merged = _gated_mm(h_bf, w_g, w_branch[i].astype(BF16), x2s=[y_conv, y_rwkv, y_hgrn],
                           b1=b_gate[i][:, None, :], act="sigmoid", out_dtype=BF16, tm=min(512, n), tn=min(512, d))
        h, h_bf = _proj_res_ln(h, merged, w_out[i].astype(BF16), ln1_g[i][None], ln1_b[i][None],
                               alpha=alpha, tm=min(256, n))

        j = i // 2
        if i % 2 == 0:
            h1 = _gated_mm(h_bf, ffn_wg[j].astype(BF16)[None], ffn_wu[j].astype(BF16)[None],
                           act="silu", out_dtype=BF16, tm=tm, tn=1024)
            terms, coef = [_matmul_ktiled(h1, ffn_wd[j].astype(BF16), tm=tm, tk=1024)], None
        else:
            terms, coef = _moe(h, h_bf, moe_router[j], moe_wg[j].astype(BF16), moe_wu[j].astype(BF16),
                               moe_wd[j].astype(BF16), tm=min(512, n))
        ple = _gated_mm(h_bf, ple_wg[i].astype(BF16)[None], ple_wp[i].astype(BF16)[None],
                        x2s=[p[i].reshape(n, -1).astype(BF16)], b1=ple_bg[i][None, None, :],
                        act="sigmoid", out_dtype=F32, tm=tm, tn=min(1024, d))
        h, h_bf = _add_ln(h, terms, coef, ple, ln2_g[i][None], ln2_b[i][None], alpha=alpha, tm=min(256, n))
    return h.reshape(bsz, seq, d)
```

```python
import functools

import jax
import jax.numpy as jnp
from jax import lax
from jax.experimental import pallas as pl
from jax.experimental.pallas import tpu as pltpu

F32 = jnp.float32
BF16 = jnp.bfloat16
HI = lax.Precision.HIGHEST

V7X_VMEM_LIMIT_BYTES = 48 * 1024 * 1024

LN_EPS = 1e-5
RWKV_LN_EPS = 64e-5
RWKV_HEAD = 64
HGRN_EXPAND = 128
CHUNK = 64
SUB = 16
CONV_HALO = 32
SUBLANES = 8
TOP_K = 2

NN_DIMS = (((1,), (0,)), ((), ()))
NT_DIMS = (((1,), (1,)), ((), ()))
TN_DIMS = (((0,), (0,)), ((), ()))


def _params(*semantics):
    return pltpu.CompilerParams(dimension_semantics=semantics,
                                vmem_limit_bytes=V7X_VMEM_LIMIT_BYTES)


def _sigmoid(z):
    return 1.0 / (1.0 + jnp.exp(-z))


def _silu(z):
    return z * _sigmoid(z)


def _act(z, kind):
    return _sigmoid(z) if kind == "sigmoid" else _silu(z)


def _layer_norm(x, g, b, eps):
    mu = jnp.mean(x, axis=-1, keepdims=True)
    xc = x - mu
    var = jnp.mean(xc * xc, axis=-1, keepdims=True)
    return xc * lax.rsqrt(var + eps) * g + b


def _split_bf16(x, terms):
    parts = []
    rest = x
    for _ in range(terms):
        part = rest.astype(BF16).astype(F32)
        parts.append(part)
        rest = rest - part
    return parts


def _bdot(a, b, kind):
    dims = {"nn": NN_DIMS, "nt": NT_DIMS, "tn": TN_DIMS}[kind]
    return lax.dot_general(a.astype(BF16), b.astype(BF16), dims, preferred_element_type=F32)


def _mask_mm(mask, x):
    return sum(jnp.dot(mask, part.astype(BF16), preferred_element_type=F32) for part in _split_bf16(x, 2))


def _mm_mask(x, mask):
    return sum(jnp.dot(part.astype(BF16), mask, preferred_element_type=F32) for part in _split_bf16(x, 2))


def _chunk_tril_mask(ts):
    row = lax.broadcasted_iota(jnp.int32, (ts, ts), 0)
    col = lax.broadcasted_iota(jnp.int32, (ts, ts), 1)
    shift = CHUNK.bit_length() - 1
    same = lax.shift_right_logical(row, shift) == lax.shift_right_logical(col, shift)
    return jnp.logical_and(same, col <= row).astype(BF16)


def _segment_ones_mask(width, seg):
    row = lax.broadcasted_iota(jnp.int32, (width, width), 0)
    col = lax.broadcasted_iota(jnp.int32, (width, width), 1)
    shift = seg.bit_length() - 1
    return (lax.shift_right_logical(row, shift) == lax.shift_right_logical(col, shift)).astype(BF16)


def _mm_kernel(x_ref, w_ref, o_ref):
    o_ref[...] = jnp.dot(x_ref[...], w_ref[...], preferred_element_type=F32).astype(o_ref.dtype)


def _matmul(x, w, *, out_dtype, tm, tn):
    m, k = x.shape
    n = w.shape[1]
    return pl.pallas_call(
        _mm_kernel,
        grid=(m // tm, n // tn),
        in_specs=[pl.BlockSpec((tm, k), lambda i, j: (i, 0)),
                  pl.BlockSpec((k, tn), lambda i, j: (0, j))],
        out_specs=pl.BlockSpec((tm, tn), lambda i, j: (i, j)),
        out_shape=jax.ShapeDtypeStruct((m, n), out_dtype),
        compiler_params=_params("parallel", "arbitrary"),
        name="matmul",
    )(x, w)


def _gated_mm_body(x1_ref, w1_ref, b1_ref, x2_refs, w2_ref, o_ref, *, nb, act):
    a = x1_ref[...]
    acc = None
    for n in range(nb):
        z = jnp.dot(a, w1_ref[n], preferred_element_type=F32)
        if b1_ref is not None:
            z = z + b1_ref[n]
        lhs2 = a if x2_refs is None else x2_refs[n][...]
        lin = jnp.dot(lhs2, w2_ref[n], preferred_element_type=F32)
        t = _act(z, act) * lin
        acc = t if acc is None else acc + t
    o_ref[...] = acc.astype(o_ref.dtype)


def _gated_mm_kernel(*refs, nb, act, has_bias, n_x2):
    refs = list(refs)
    x1_ref, w1_ref = refs[0], refs[1]
    pos = 2
    b1_ref = None
    if has_bias:
        b1_ref = refs[pos]
        pos += 1
    x2_refs = refs[pos:pos + n_x2] if n_x2 else None
    pos += n_x2
    w2_ref, o_ref = refs[pos], refs[pos + 1]
    _gated_mm_body(x1_ref, w1_ref, b1_ref, x2_refs, w2_ref, o_ref, nb=nb, act=act)


def _gated_mm(x1, w1, w2, *, x2s=None, b1=None, act, out_dtype, tm, tn):
    m, k1 = x1.shape
    nb, _, n = w1.shape
    k2 = w2.shape[1]
    in_specs = [pl.BlockSpec((tm, k1), lambda i, j: (i, 0)),
                pl.BlockSpec((nb, k1, tn), lambda i, j: (0, 0, j))]
    args = [x1, w1]
    if b1 is not None:
        in_specs.append(pl.BlockSpec((nb, 1, tn), lambda i, j: (0, 0, j)))
        args.append(b1)
    n_x2 = 0
    if x2s is not None:
        n_x2 = len(x2s)
        for x2 in x2s:
            in_specs.append(pl.BlockSpec((tm, k2), lambda i, j: (i, 0)))
            args.append(x2)
    in_specs.append(pl.BlockSpec((nb, k2, tn), lambda i, j: (0, 0, j)))
    args.append(w2)
    return pl.pallas_call(
        functools.partial(_gated_mm_kernel, nb=nb, act=act, has_bias=b1 is not None, n_x2=n_x2),
        grid=(m // tm, n // tn),
        in_specs=in_specs,
        out_specs=pl.BlockSpec((tm, tn), lambda i, j: (i, j)),
        out_shape=jax.ShapeDtypeStruct((m, n), out_dtype),
        compiler_params=_params("parallel", "arbitrary"),
        name="gated_matmul",
    )(*args)


def _grouped_gated_kernel(te_ref, nt_ref, x_ref, w1_ref, w2_ref, o_ref, w1_bf, w2_bf, *, act):
    i = pl.program_id(1)
    new_expert = jnp.logical_or(i == 0, te_ref[i] != te_ref[jnp.maximum(i - 1, 0)])

    @pl.when(new_expert)
    def _():
        w1_bf[...] = w1_ref[0].astype(BF16)
        w2_bf[...] = w2_ref[0].astype(BF16)

    @pl.when(i < nt_ref[0])
    def _():
        a = x_ref[...]
        z = jnp.dot(a, w1_bf[...], preferred_element_type=F32)
        lin = jnp.dot(a, w2_bf[...], preferred_element_type=F32)
        o_ref[...] = (_act(z, act) * lin).astype(o_ref.dtype)

    @pl.when(i >= nt_ref[0])
    def _():
        o_ref[...] = jnp.zeros_like(o_ref)


def _grouped_gated_mm(x, w1, w2, tile_expert, n_tiles, *, act, out_dtype, tm, tn):
    m, k = x.shape
    n = w1.shape[2]
    grid_spec = pltpu.PrefetchScalarGridSpec(
        num_scalar_prefetch=2,
        grid=(n // tn, m // tm),
        in_specs=[pl.BlockSpec((tm, k), lambda j, i, te, nt: (i, 0)),
                  pl.BlockSpec((1, k, tn), lambda j, i, te, nt: (te[i], 0, j)),
                  pl.BlockSpec((1, k, tn), lambda j, i, te, nt: (te[i], 0, j))],
        out_specs=pl.BlockSpec((tm, tn), lambda j, i, te, nt: (i, j)),
        scratch_shapes=[pltpu.VMEM((k, tn), BF16), pltpu.VMEM((k, tn), BF16)],
    )
    return pl.pallas_call(
        functools.partial(_grouped_gated_kernel, act=act),
        grid_spec=grid_spec,
        out_shape=jax.ShapeDtypeStruct((m, n), out_dtype),
        compiler_params=_params("arbitrary", "arbitrary"),
        name="grouped_gated_matmul",
    )(tile_expert, n_tiles, x, w1, w2)


def _grouped_mm_kernel(te_ref, nt_ref, x_ref, w_ref, o_ref):
    i = pl.program_id(0)

    @pl.when(i < nt_ref[0])
    def _():
        o_ref[...] = jnp.dot(x_ref[...], w_ref[0], preferred_element_type=F32)

    @pl.when(i >= nt_ref[0])
    def _():
        o_ref[...] = jnp.zeros_like(o_ref)


def _grouped_matmul(x, w, tile_expert, n_tiles, *, tm, tn):
    m, k = x.shape
    n = w.shape[2]
    grid_spec = pltpu.PrefetchScalarGridSpec(
        num_scalar_prefetch=2,
        grid=(m // tm, n // tn),
        in_specs=[pl.BlockSpec((tm, k), lambda i, j, te, nt: (i, 0)),
                  pl.BlockSpec((1, k, tn), lambda i, j, te, nt: (te[i], 0, j))],
        out_specs=pl.BlockSpec((tm, tn), lambda i, j, te, nt: (i, j)),
    )
    return pl.pallas_call(
        _grouped_mm_kernel,
        grid_spec=grid_spec,
        out_shape=jax.ShapeDtypeStruct((m, n), F32),
        compiler_params=_params("arbitrary", "arbitrary"),
        name="grouped_matmul",
    )(tile_expert, n_tiles, x, w)


def _proj_res_ln_kernel(h_ref, m_ref, w_ref, g_ref, b_ref, o_ref, obf_ref, *, alpha):
    y = alpha * h_ref[...] + jnp.dot(m_ref[...], w_ref[...], preferred_element_type=F32)
    out = _layer_norm(y, g_ref[...], b_ref[...], LN_EPS)
    o_ref[...] = out
    obf_ref[...] = out.astype(BF16)


def _proj_res_ln(h, m_in, w, g, b, *, alpha, tm):
    n, d = h.shape
    k = m_in.shape[1]
    return pl.pallas_call(
        functools.partial(_proj_res_ln_kernel, alpha=alpha),
        grid=(n // tm,),
        in_specs=[pl.BlockSpec((tm, d), lambda i: (i, 0)),
                  pl.BlockSpec((tm, k), lambda i: (i, 0)),
                  pl.BlockSpec((k, d), lambda i: (0, 0)),
                  pl.BlockSpec((1, d), lambda i: (0, 0)),
                  pl.BlockSpec((1, d), lambda i: (0, 0))],
        out_specs=[pl.BlockSpec((tm, d), lambda i: (i, 0)),
                   pl.BlockSpec((tm, d), lambda i: (i, 0))],
        out_shape=[jax.ShapeDtypeStruct((n, d), F32), jax.ShapeDtypeStruct((n, d), BF16)],
        compiler_params=_params("parallel"),
        name="proj_res_ln",
    )(h, m_in, w, g, b)


def _add_ln_kernel(*refs, alpha, n_terms, weighted):
    h_ref = refs[0]
    a_refs = refs[1:1 + n_terms]
    pos = 1 + n_terms
    c_ref = None
    if weighted:
        c_ref = refs[pos]
        pos += 1
    ple_ref, g_ref, b_ref, o_ref, obf_ref = refs[pos:pos + 5]
    y = alpha * h_ref[...] + ple_ref[...]
    for j in range(n_terms):
        t = a_refs[j][...]
        if weighted:
            t = t * c_ref[:, j:j + 1]
        y = y + t
    out = _layer_norm(y, g_ref[...], b_ref[...], LN_EPS)
    o_ref[...] = out
    obf_ref[...] = out.astype(BF16)


def _add_ln(h, terms, coef, ple, g, b, *, alpha, tm):
    n, d = h.shape
    row = pl.BlockSpec((tm, d), lambda i: (i, 0))
    vec = pl.BlockSpec((1, d), lambda i: (0, 0))
    in_specs = [row] + [row] * len(terms)
    args = [h] + list(terms)
    if coef is not None:
        in_specs.append(pl.BlockSpec((tm, coef.shape[1]), lambda i: (i, 0)))
        args.append(coef)
    in_specs += [row, vec, vec]
    args += [ple, g, b]
    return pl.pallas_call(
        functools.partial(_add_ln_kernel, alpha=alpha, n_terms=len(terms), weighted=coef is not None),
        grid=(n // tm,),
        in_specs=in_specs,
        out_specs=[row, row],
        out_shape=[jax.ShapeDtypeStruct((n, d), F32), jax.ShapeDtypeStruct((n, d), BF16)],
        compiler_params=_params("parallel"),
        name="add_ln",
    )(*args)


def _conv_kernel(u_ref, dw_ref, dwb_ref, g_ref, b_ref, o_ref, buf, shifted, *, ts, rb, width):
    s = pl.program_id(1)

    @pl.when(s == 0)
    def _():
        buf[0:CONV_HALO, :] = jnp.zeros((CONV_HALO, buf.shape[1]), F32)

    buf[CONV_HALO:CONV_HALO + ts, :] = u_ref[0]
    lead = CONV_HALO - (width - 1)
    span = shifted.shape[1]
    for r in range(1, SUBLANES):
        shifted[r - 1] = buf[r:r + span, :]
    for r0 in range(0, ts, rb):
        acc = jnp.broadcast_to(dwb_ref[...], (rb, buf.shape[1]))
        for j in range(width):
            m, r = divmod(lead + j, SUBLANES)
            lo = r0 + SUBLANES * m
            rows = buf[lo:lo + rb, :] if r == 0 else shifted[r - 1, lo:lo + rb, :]
            acc = acc + rows * dw_ref[SUBLANES * j:SUBLANES * (j + 1), :]
        y = _silu(_layer_norm(acc, g_ref[...], b_ref[...], LN_EPS))
        o_ref[0, r0:r0 + rb, :] = y.astype(o_ref.dtype)
    buf[0:CONV_HALO, :] = buf[ts:ts + CONV_HALO, :]


def _conv_module(u, dw, dw_b, ln_g, ln_b, *, ts, rb):
    bsz, seq, c = u.shape
    width = dw.shape[0]
    assert rb == SUBLANES
    dw_rep = jnp.repeat(dw.astype(F32), SUBLANES, axis=0)
    vec = pl.BlockSpec((1, c), lambda b, s: (0, 0))
    return pl.pallas_call(
        functools.partial(_conv_kernel, ts=ts, rb=rb, width=width),
        grid=(bsz, seq // ts),
        in_specs=[pl.BlockSpec((1, ts, c), lambda b, s: (b, s, 0)),
                  pl.BlockSpec((width * SUBLANES, c), lambda b, s: (0, 0)),
                  vec, vec, vec],
        out_specs=pl.BlockSpec((1, ts, c), lambda b, s: (b, s, 0)),
        out_shape=jax.ShapeDtypeStruct((bsz, seq, c), BF16),
        scratch_shapes=[pltpu.VMEM((CONV_HALO + ts, c), F32),
                        pltpu.VMEM((SUBLANES - 1, CONV_HALO + ts - SUBLANES, c), F32)],
        compiler_params=_params("parallel", "arbitrary"),
        name="conv_module",
    )(u, dw_rep, dw_b.reshape(1, c), ln_g.reshape(1, c), ln_b.reshape(1, c))


def _hgrn_kernel(zq_ref, zf_ref, zi_ref, zg_ref, lb_ref, ng_ref, o_ref, state_t, k_buf, b_buf, *, ts):
    s = pl.program_id(2)
    nc = ts // CHUNK
    nsub = CHUNK // SUB

    @pl.when(s == 0)
    def _():
        state_t[...] = jnp.zeros_like(state_t)

    q = _silu(zq_ref[0]) * (HGRN_EXPAND ** -0.5)
    k = (1.0 - lb_ref[...]) * _sigmoid(-zf_ref[0])
    v = zi_ref[0]
    b = _mask_mm(_chunk_tril_mask(ts), jnp.log(1.0 - k))
    k_buf[...] = k
    b_buf[...] = b

    row = lax.broadcasted_iota(jnp.int32, (CHUNK, CHUNK), 0)
    col = lax.broadcasted_iota(jnp.int32, (CHUNK, CHUNK), 1)
    row_sub = lax.shift_right_logical(row, SUB.bit_length() - 1)
    col_sub = lax.shift_right_logical(col, SUB.bit_length() - 1)
    earlier_sub = col_sub < row_sub
    same_sub_causal = jnp.logical_and(col_sub == row_sub, col <= row)
    strip_col = lax.broadcasted_iota(jnp.int32, (SUB, CHUNK), 1)

    o_intra, upd, decay, q_dec = [], [], [], []
    for c in range(nc):
        base = c * CHUNK
        q_c, k_c, b_c, v_c = (t[base:base + CHUNK] for t in (q, k, b, v))
        off_strips, diag_strips = [], []
        for i in range(nsub):
            lo = base + i * SUB
            q_i, b_i = q[lo:lo + SUB], b[lo:lo + SUB]
            if i == 0:
                off_strips.append(jnp.zeros((SUB, CHUNK), F32))
            else:
                beta = b_buf[lo - 1:lo, :]
                q_hat = q_i * jnp.exp(b_i - beta)
                k_hat = k_c * jnp.exp(jnp.minimum(beta - b_c, 0.0))
                off_strips.append(_bdot(q_hat, k_hat, "nt"))
            strip = jnp.zeros((SUB, CHUNK), F32)
            for j in range(SUB):
                sj = i * SUB + j
                k_row = k_buf[lo + j:lo + j + 1, :]
                b_row = b_buf[lo + j:lo + j + 1, :]
                e = jnp.exp(jnp.minimum(b_i - b_row, 0.0))
                colv = jnp.sum(q_i * (k_row * e), axis=-1, keepdims=True)
                strip = jnp.where(strip_col == sj, colv, strip)
            diag_strips.append(strip)
        att = (jnp.where(earlier_sub, jnp.concatenate(off_strips, axis=0), 0.0)
               + jnp.where(same_sub_causal, jnp.concatenate(diag_strips, axis=0), 0.0))
        o_intra.append(_bdot(att, v_c, "nn"))
        b_last = b_buf[base + CHUNK - 1:base + CHUNK, :]
        upd.append(_bdot(v_c, k_c * jnp.exp(b_last - b_c), "tn"))
        decay.append(jnp.exp(b_last))
        q_dec.append(q_c * jnp.exp(b_c))

    st = state_t[...]
    outs = []
    for c in range(nc):
        outs.append(o_intra[c] + _bdot(q_dec[c], st, "nt"))
        st = st * decay[c] + upd[c]
    state_t[...] = st
    o = jnp.concatenate(outs, axis=0)
    o = o * lax.rsqrt(jnp.mean(o * o, axis=-1, keepdims=True) + LN_EPS) * ng_ref[...]
    o_ref[0] = (o * _silu(zg_ref[0])).astype(o_ref.dtype)


def _hgrn_mix(z, lb, norm_g, *, ts):
    bsz, seq, c4 = z.shape
    dh = HGRN_EXPAND
    heads = c4 // (4 * dh)

    def zspec(group):
        return pl.BlockSpec((1, ts, dh), lambda b, h, s: (b, s, group * heads + h))

    return pl.pallas_call(
        functools.partial(_hgrn_kernel, ts=ts),
        grid=(bsz, heads, seq // ts),
        in_specs=[zspec(0), zspec(1), zspec(2), zspec(3),
                  pl.BlockSpec((1, dh), lambda b, h, s: (0, h)),
                  pl.BlockSpec((1, dh), lambda b, h, s: (0, 0))],
        out_specs=pl.BlockSpec((1, ts, dh), lambda b, h, s: (b, s, h)),
        out_shape=jax.ShapeDtypeStruct((bsz, seq, heads * dh), BF16),
        scratch_shapes=[pltpu.VMEM((dh, dh), F32),
                        pltpu.VMEM((ts, dh), F32),
                        pltpu.VMEM((ts, dh), F32)],
        compiler_params=_params("parallel", "parallel", "arbitrary"),
        name="hgrn_mix",
    )(z, z, z, z, lb.reshape(1, heads * dh), norm_g.reshape(1, dh))


def _token_shift(z, prev_row, mu):
    rows = lax.broadcasted_iota(jnp.int32, z.shape, 0)
    shifted = jnp.where(rows == 0, prev_row, pltpu.roll(z, 1, axis=0))
    return z + (shifted - z) * mu


def _rwkv_prep_kernel(*refs, vres, c):
    (zrkv_ref, zl_ref, mu_rkv_ref, mu_l_ref, w0_ref, w2_ref, a0_ref, a2_ref, g2_ref,
     kk_ref, ka_ref) = refs[:11]
    pos = 11
    if vres:
        v0_ref, v2_ref, vfirst_ref = refs[pos:pos + 3]
        pos += 3
    r_o, k_o, v_o, ld_o, kk_o, a_o, g_o, prev_rkv, prev_l = refs[pos:pos + 9]
    s = pl.program_id(1)

    @pl.when(s == 0)
    def _():
        prev_rkv[...] = jnp.zeros_like(prev_rkv)
        prev_l[...] = jnp.zeros_like(prev_l)

    z = zrkv_ref[0]
    zl = zl_ref[0]
    ts = z.shape[0]
    z_mix = _token_shift(z, prev_rkv[...], mu_rkv_ref[...])
    zl_mix = _token_shift(zl, prev_l[...], mu_l_ref[...])
    prev_rkv[...] = z[ts - 1:ts, :]
    prev_l[...] = zl[ts - 1:ts, :]

    zr, zk, zv = z_mix[:, 0:c], z_mix[:, c:2 * c], z_mix[:, 2 * c:3 * c]
    zw, za, zg = zl_mix[:, 0:128], zl_mix[:, 128:256], zl_mix[:, 256:512]

    def lora(act_in, w_ref):
        return jnp.dot(act_in.astype(BF16), w_ref[...], preferred_element_type=F32)

    x = -(w0_ref[...] + lora(jnp.tanh(zw), w2_ref))
    softplus = jnp.maximum(x, 0.0) + jnp.log(1.0 + jnp.exp(-jnp.abs(x)))
    w = -softplus - 0.5
    a = _sigmoid(a0_ref[...] + lora(za, a2_ref))
    g = lora(_sigmoid(zg), g2_ref)
    v = zv
    if vres:
        zm = zl_mix[:, 512:640]
        v = zv + (vfirst_ref[0] - zv) * _sigmoid(v0_ref[...] + lora(zm, v2_ref))
    r_o[0] = zr
    k_o[0] = zk * (1.0 + (a - 1.0) * ka_ref[...])
    v_o[0] = v
    ld_o[0] = -jnp.exp(w)
    kk_o[0] = zk * kk_ref[...]
    a_o[0] = a
    g_o[0] = g


def _rwkv_prep(z_rkv, z_lora, mu_rkv, mu_l, w0, w2, a0, a2, g2, k_k, k_a, v0, v2, v_first, *, ts):
    bsz, seq, c3 = z_rkv.shape
    c = c3 // 3
    lw = z_lora.shape[2]
    vres = v0 is not None

    def full(arr):
        return pl.BlockSpec(arr.shape, lambda b, s: (0,) * arr.ndim)

    tile = pl.BlockSpec((1, ts, c), lambda b, s: (b, s, 0))
    args = [z_rkv, z_lora, mu_rkv, mu_l, w0, w2, a0, a2, g2, k_k, k_a]
    in_specs = [pl.BlockSpec((1, ts, c3), lambda b, s: (b, s, 0)),
                pl.BlockSpec((1, ts, lw), lambda b, s: (b, s, 0))] + [full(a) for a in args[2:]]
    if vres:
        args += [v0, v2, v_first]
        in_specs += [full(v0), full(v2), tile]
    return pl.pallas_call(
        functools.partial(_rwkv_prep_kernel, vres=vres, c=c),
        grid=(bsz, seq // ts),
        in_specs=in_specs,
        out_specs=[tile] * 7,
        out_shape=[jax.ShapeDtypeStruct((bsz, seq, c), F32)] * 7,
        scratch_shapes=[pltpu.VMEM((1, c3), F32), pltpu.VMEM((1, lw), F32)],
        compiler_params=_params("parallel", "arbitrary"),
        name="rwkv_prep",
    )(*args)


def _rwkv_scan_kernel(r_ref, k_ref, v_ref, ld_ref, kk_ref, a_ref, g_ref, rk_ref, lng_ref, lnb_ref,
                      o_ref, state, *, ts, hg):
    s = pl.program_id(2)
    n = RWKV_HEAD
    nc = ts // CHUNK
    width = hg * n

    @pl.when(s == 0)
    def _():
        state[...] = jnp.zeros_like(state)

    r, k, v, ld, kk, a = (ref[0] for ref in (r_ref, k_ref, v_ref, ld_ref, kk_ref, a_ref))
    head_ones = _segment_ones_mask(width, n)
    lp = _mask_mm(_chunk_tril_mask(ts), ld)
    norm = jnp.sqrt(_mm_mask(kk * kk, head_ones))
    kkn = kk / jnp.maximum(norm, 1e-12)
    bvec = kkn * a
    inv_p = jnp.exp(-lp)
    a_t = -kkn * jnp.exp(lp - ld)
    r_t = r * jnp.exp(lp)
    b_t = bvec * inv_p
    k_t = k * inv_p

    row = lax.broadcasted_iota(jnp.int32, (CHUNK, CHUNK), 0)
    col = lax.broadcasted_iota(jnp.int32, (CHUNK, CHUNK), 1)
    incl = row >= col
    strict = row > col
    eye = (row == col).astype(F32)

    cells = [(c, hh) for c in range(nc) for hh in range(hg)]

    def blk(x, c, hh):
        return x[c * CHUNK:(c + 1) * CHUNK, hh * n:(hh + 1) * n]

    p_last = [jnp.exp(lp[(c + 1) * CHUNK - 1:(c + 1) * CHUNK, :]) for c in range(nc)]
    to_end = jnp.concatenate(
        [jnp.exp(lp[(c + 1) * CHUNK - 1:(c + 1) * CHUNK, :] - lp[c * CHUNK:(c + 1) * CHUNK, :]) for c in range(nc)],
        axis=0)
    b_h = bvec * to_end
    k_h = k * to_end

    a_c = {cell: blk(a_t, *cell) for cell in cells}
    v_c = {cell: blk(v, *cell) for cell in cells}
    gram = {cell: _bdot(jnp.concatenate([a_c[cell], blk(r_t, *cell)], axis=0),
                       jnp.concatenate([blk(b_t, *cell), blk(k_t, *cell)], axis=0), "nt") for cell in cells}
    t_ab = {cell: jnp.where(strict, gram[cell][0:CHUNK, 0:CHUNK], 0.0) for cell in cells}
    t_ak = {cell: jnp.where(strict, gram[cell][0:CHUNK, CHUNK:], 0.0) for cell in cells}
    t_rb = {cell: jnp.where(incl, gram[cell][CHUNK:, 0:CHUNK], 0.0) for cell in cells}
    t_rk = {cell: jnp.where(incl, gram[cell][CHUNK:, CHUNK:], 0.0) for cell in cells}

    def square(p):
        return {cell: _bdot(p[cell], p[cell], "nn") for cell in cells}

    def pair(p_lo, p_hi):
        return {cell: eye + p_lo[cell] + _bdot(eye + p_lo[cell], p_hi[cell], "nn") for cell in cells}

    p2 = square(t_ab)
    f01 = pair(t_ab, p2)
    p4 = square(p2)
    p8 = square(p4)
    f23 = pair(p4, p8)
    f03 = {cell: _bdot(f01[cell], f23[cell], "nn") for cell in cells}
    p16 = square(p8)
    p32 = square(p16)
    f45 = pair(p16, p32)
    inv = {cell: _bdot(f03[cell], f45[cell], "nn") for cell in cells}

    tv = {cell: _bdot(jnp.concatenate([t_ak[cell], t_rk[cell]], axis=0), v_c[cell], "nn") for cell in cells}
    wu = {cell: _bdot(inv[cell], jnp.concatenate([a_c[cell], tv[cell][0:CHUNK]], axis=1), "nn") for cell in cells}
    qy = {cell: _bdot(t_rb[cell], wu[cell], "nn") for cell in cells}
    gh = {cell: _bdot(blk(b_h, *cell), wu[cell], "tn") for cell in cells}
    kv = {cell: _bdot(blk(k_h, *cell), v_c[cell], "tn") for cell in cells}
    q_g, y0, h_t = {}, {}, {}
    for cell in cells:
        c, hh = cell
        q_eff = blk(r_t, *cell) + qy[cell][:, 0:n]
        g_t = eye * p_last[c][:, hh * n:(hh + 1) * n] + gh[cell][:, 0:n]
        q_g[cell] = jnp.concatenate([q_eff, g_t], axis=0)
        y0[cell] = qy[cell][:, n:] + tv[cell][CHUNK:]
        h_t[cell] = gh[cell][:, n:] + kv[cell]

    z = [state[hh] for hh in range(hg)]
    y_rows = []
    for c in range(nc):
        y_heads = []
        for hh in range(hg):
            res = _bdot(q_g[(c, hh)], z[hh], "nn")
            y_heads.append(res[0:CHUNK] + y0[(c, hh)])
            z[hh] = res[CHUNK:] + h_t[(c, hh)]
        y_rows.append(jnp.concatenate(y_heads, axis=1))
    for hh in range(hg):
        state[hh] = z[hh]
    y = jnp.concatenate(y_rows, axis=0)

    inv_n = 1.0 / n
    mu = _mm_mask(y, head_ones) * inv_n
    yc = y - mu
    var = _mm_mask(yc * yc, head_ones) * inv_n
    yn = yc * lax.rsqrt(var + RWKV_LN_EPS) * lng_ref[...] + lnb_ref[...]
    bonus = _mm_mask(r * k * rk_ref[...], head_ones) * v
    o_ref[0] = ((yn + bonus) * g_ref[0]).astype(o_ref.dtype)


def _rwkv_scan(r, k, v, ld, kk, a, g, r_k, lnx_g, lnx_b, *, ts, hg):
    bsz, seq, c = r.shape
    wblk = hg * RWKV_HEAD
    tile = pl.BlockSpec((1, ts, wblk), lambda b, h, s: (b, s, h))
    vec = pl.BlockSpec((1, wblk), lambda b, h, s: (0, h))
    return pl.pallas_call(
        functools.partial(_rwkv_scan_kernel, ts=ts, hg=hg),
        grid=(bsz, c // wblk, seq // ts),
        in_specs=[tile] * 7 + [vec] * 3,
        out_specs=tile,
        out_shape=jax.ShapeDtypeStruct((bsz, seq, c), BF16),
        scratch_shapes=[pltpu.VMEM((hg, RWKV_HEAD, RWKV_HEAD), F32)],
        compiler_params=_params("parallel", "parallel", "arbitrary"),
        name="rwkv_scan",
    )(r, k, v, ld, kk, a, g, r_k.reshape(1, c), lnx_g.reshape(1, c), lnx_b.reshape(1, c))


def _router_kernel(h_ref, w_ref, idx_ref, wt_ref, *, n_experts):
    logits = jnp.dot(h_ref[...], w_ref[...], precision=HI, preferred_element_type=F32)
    lane = lax.broadcasted_iota(jnp.int32, logits.shape, 1)
    neg = jnp.float32(-jnp.inf)
    big = jnp.int32(logits.shape[1])
    x = jnp.where(lane < n_experts, logits, neg)
    m1 = jnp.max(x, axis=-1, keepdims=True)
    i1 = jnp.min(jnp.where(x == m1, lane, big), axis=-1, keepdims=True)
    x2 = jnp.where(lane == i1, neg, x)
    m2 = jnp.max(x2, axis=-1, keepdims=True)
    i2 = jnp.min(jnp.where(x2 == m2, lane, big), axis=-1, keepdims=True)
    e2 = jnp.exp(m2 - m1)
    w1 = 1.0 / (1.0 + e2)
    w2 = e2 / (1.0 + e2)
    idx_ref[...] = jnp.where(lane == 0, i1, jnp.where(lane == 1, i2, 0))
    wt_ref[...] = jnp.where(lane == 0, w1, jnp.where(lane == 1, w2, 0.0))


def _router(h, w_router, *, tm):
    n, d = h.shape
    n_experts = w_router.shape[1]
    w_pad = jnp.zeros((d, 128), F32).at[:, :n_experts].set(w_router)
    out = pl.BlockSpec((tm, 128), lambda i: (i, 0))
    return pl.pallas_call(
        functools.partial(_router_kernel, n_experts=n_experts),
        grid=(n // tm,),
        in_specs=[pl.BlockSpec((tm, d), lambda i: (i, 0)),
                  pl.BlockSpec((d, 128), lambda i: (0, 0))],
        out_specs=[out, out],
        out_shape=[jax.ShapeDtypeStruct((n, 128), jnp.int32), jax.ShapeDtypeStruct((n, 128), F32)],
        compiler_params=_params("parallel"),
        name="moe_router",
    )(h, w_pad)


def _moe(h, h_bf, w_router, wg, wu, wd, *, tm):
    n, d = h.shape
    n_experts = wg.shape[0]
    idx_pad, wt_pad = _router(h, w_router, tm=512)
    e_flat = idx_pad[:, :TOP_K].reshape(-1)
    onehot = (e_flat[:, None] == jnp.arange(n_experts, dtype=jnp.int32)[None, :]).astype(jnp.int32)
    csum = jnp.cumsum(onehot, axis=0)
    rank = jnp.sum(csum * onehot, axis=1) - 1
    counts = csum[-1]
    padded = ((counts + tm - 1) // tm) * tm
    ends = jnp.cumsum(padded)
    starts = ends - padded
    pos = starts[e_flat] + rank
    n_rows = TOP_K * n + n_experts * tm
    src = jnp.zeros((n_rows,), jnp.int32).at[pos].set(jnp.arange(TOP_K * n, dtype=jnp.int32) // TOP_K)
    n_tiles = (ends[-1] // tm).astype(jnp.int32).reshape(1)
    tile_start = jnp.arange(n_rows // tm, dtype=jnp.int32) * tm
    tile_expert = jnp.minimum(jnp.searchsorted(ends, tile_start, side="right"), n_experts - 1).astype(jnp.int32)
    x_sorted = h_bf.at[src].get(mode="promise_in_bounds")
    h1 = _grouped_gated_mm(x_sorted, wg, wu, tile_expert, n_tiles, act="silu", out_dtype=BF16, tm=tm, tn=512)
    y_sorted = _grouped_matmul(h1, wd, tile_expert, n_tiles, tm=tm, tn=512)
    pos2 = pos.reshape(n, TOP_K)
    y0 = y_sorted.at[pos2[:, 0]].get(mode="promise_in_bounds")
    y1 = y_sorted.at[pos2[:, 1]].get(mode="promise_in_bounds")
    return [y0, y1], wt_pad


def _pad_cols(w, width):
    return jnp.pad(w, ((0, 0), (0, width - w.shape[1])))


def _pad_rows(w, height):
    return jnp.pad(w, ((0, height - w.shape[0]), (0, 0)))


def kernel(x, p, w_in, w_in_vres, conv_dw, conv_dw_b, conv_ln_g, conv_ln_b, rwkv_mu, rwkv_mu_vres, rwkv_w0, rwkv_w2, rwkv_a0, rwkv_a2, rwkv_v0, rwkv_v2, rwkv_g2, rwkv_kk, rwkv_ka, rwkv_rk, rwkv_lnx_g, rwkv_lnx_b, hgrn_lb, hgrn_norm_g, w_branch, b_gate, w_out, ln1_g, ln1_b, ffn_wg, ffn_wu, ffn_wd, moe_router, moe_wg, moe_wu, moe_wd, ple_wp, ple_wg, ple_bg, ln2_g, ln2_b):
    bsz, seq, d = x.shape
    depth = w_in.shape[0]
    n = bsz * seq
    cc = conv_dw.shape[2]
    cr = rwkv_w0.shape[1]
    ch = hgrn_lb.shape[1]
    d_decay, d_aaa, d_gate, d_mv = rwkv_w2.shape[1], rwkv_a2.shape[1], rwkv_g2.shape[1], rwkv_v2.shape[1]
    n_branch = w_branch.shape[1]
    alpha = float((2 * depth) ** 0.25)
    tm = min(1024, n)
    ts = min(256, seq)

    lb_soft = jax.nn.softmax(hgrn_lb.astype(F32), axis=0)
    lower_bounds = jnp.cumsum(lb_soft, axis=0) - lb_soft[0]

    h = x.reshape(n, d)
    h_bf = h.astype(BF16)
    v_first = None
    for i in range(depth):
        vres = i > 0
        wi = w_in[i]
        o_rwkv = 2 * cc
        o_lora = o_rwkv + 3 * cr
        o_hgrn = o_lora + d_decay + d_aaa + d_gate
        o_gate = o_hgrn + 4 * ch

        w_val = wi[:, 0:cc].astype(BF16)[None]
        w_glu = wi[:, cc:2 * cc].astype(BF16)[None]
        u = _gated_mm(h_bf, w_glu, w_val, act="sigmoid", out_dtype=F32, tm=tm, tn=min(1024, cc))
        y_conv = _conv_module(u.reshape(bsz, seq, cc), conv_dw[i], conv_dw_b[i], conv_ln_g[i], conv_ln_b[i],
                              ts=ts, rb=SUBLANES).reshape(n, cc)

        z_rkv = _matmul(h_bf, wi[:, o_rwkv:o_lora].astype(BF16), out_dtype=F32, tm=tm, tn=min(1024, cr))
        lora_cols = [_pad_cols(wi[:, o_lora:o_lora + d_decay], 128),
                     _pad_cols(wi[:, o_lora + d_decay:o_lora + d_decay + d_aaa], 128),
                     _pad_cols(wi[:, o_lora + d_decay + d_aaa:o_hgrn], 256)]
        mu_i = rwkv_mu[i]
        mu_cols = [_pad_cols(mu_i[None, 3 * cr:3 * cr + d_decay], 128),
                   _pad_cols(mu_i[None, 3 * cr + d_decay:3 * cr + d_decay + d_aaa], 128),
                   _pad_cols(mu_i[None, 3 * cr + d_decay + d_aaa:], 256)]
        if vres:
            lora_cols.append(_pad_cols(w_in_vres[i - 1], 128))
            mu_cols.append(_pad_cols(rwkv_mu_vres[i - 1][None], 128))
        w_lora = jnp.concatenate(lora_cols, axis=1).astype(BF16)
        mu_l = jnp.concatenate(mu_cols, axis=1)
        z_lora = _matmul(h_bf, w_lora, out_dtype=F32, tm=tm, tn=w_lora.shape[1])
        r, k, v, ld, kk, a, g = _rwkv_prep(
            z_rkv.reshape(bsz, seq, 3 * cr), z_lora.reshape(bsz, seq, -1),
            mu_i[None, :3 * cr], mu_l,
            rwkv_w0[i][None], _pad_rows(rwkv_w2[i], 128).astype(BF16),
            rwkv_a0[i][None], _pad_rows(rwkv_a2[i], 128).astype(BF16),
            _pad_rows(rwkv_g2[i], 256).astype(BF16),
            rwkv_kk[i][None], rwkv_ka[i][None],
            rwkv_v0[i - 1][None] if vres else None,
            _pad_rows(rwkv_v2[i - 1], 128).astype(BF16) if vres else None,
            v_first, ts=ts)
        if not vres:
            v_first = v
        y_rwkv = _rwkv_scan(r, k, v, ld, kk, a, g, rwkv_rk[i], rwkv_lnx_g[i], rwkv_lnx_b[i],
                            ts=ts, hg=4).reshape(n, cr)

        z_hgrn = _matmul(h_bf, wi[:, o_hgrn:o_gate].astype(BF16), out_dtype=F32, tm=tm, tn=min(1024, 4 * ch))
        y_hgrn = _hgrn_mix(z_hgrn.reshape(bsz, seq, 4 * ch), lower_bounds[i], hgrn_norm_g[i],
                           ts=min(512, seq)).reshape(n, ch)

        w_g = wi[:, o_gate:o_gate + n_branch * d].reshape(d, n_branch, d).transpose(1, 0, 2).astype(BF16)
        merged = _gated_mm(h_bf, w_g, w_branch[i].astype(BF16), x2s=[y_conv, y_rwkv, y_hgrn],
                           b1=b_gate[i][:, None, :], act="sigmoid", out_dtype=BF16, tm=min(512, n), tn=min(512, d))
        h, h_bf = _proj_res_ln(h, merged, w_out[i].astype(BF16), ln1_g[i][None], ln1_b[i][None],
                               alpha=alpha, tm=min(256, n))

        j = i // 2
        if i % 2 == 0:
            h1 = _gated_mm(h_bf, ffn_wg[j].astype(BF16)[None], ffn_wu[j].astype(BF16)[None],
                           act="silu", out_dtype=BF16, tm=tm, tn=1024)
            terms, coef = [_matmul(h1, ffn_wd[j].astype(BF16), out_dtype=F32, tm=min(512, n), tn=512)], None
        else:
            terms, coef = _moe(h, h_bf, moe_router[j], moe_wg[j], moe_wu[j],
                               moe_wd[j].astype(BF16), tm=min(512, n))
        ple = _gated_mm(h_bf, ple_wg[i].astype(BF16)[None], ple_wp[i].astype(BF16)[None],
                        x2s=[p[i].reshape(n, -1).astype(BF16)], b1=ple_bg[i][None, None, :],
                        act="sigmoid", out_dtype=F32, tm=tm, tn=min(1024, d))
        h, h_bf = _add_ln(h, terms, coef, ple, ln2_g[i][None], ln2_b[i][None], alpha=alpha, tm=min(256, n))
    return h.reshape(bsz, seq, d)
```

```python
import functools

import jax
import jax.numpy as jnp
from jax import lax
from jax.experimental import pallas as pl
from jax.experimental.pallas import tpu as pltpu

F32 = jnp.float32
BF16 = jnp.bfloat16
HI = lax.Precision.HIGHEST

V7X_VMEM_LIMIT_BYTES = 48 * 1024 * 1024

LN_EPS = 1e-5
RWKV_LN_EPS = 64e-5
RWKV_HEAD = 64
HGRN_EXPAND = 128
CHUNK = 64
SUB = 16
CONV_HALO = 32
SUBLANES = 8
TOP_K = 2
SCAN_GROUPS = 4
CELLS_PER_YIELD = 4

NN_DIMS = (((1,), (0,)), ((), ()))
NT_DIMS = (((1,), (1,)), ((), ()))
TN_DIMS = (((0,), (0,)), ((), ()))


def _params(*semantics, vmem_limit_bytes=V7X_VMEM_LIMIT_BYTES):
    return pltpu.CompilerParams(dimension_semantics=semantics, vmem_limit_bytes=vmem_limit_bytes)


def _sigmoid(z):
    return 1.0 / (1.0 + jnp.exp(-z))


def _silu(z):
    return z * _sigmoid(z)


def _act(z, kind):
    return _sigmoid(z) if kind == "sigmoid" else _silu(z)


def _layer_norm(x, g, b, eps):
    mu = jnp.mean(x, axis=-1, keepdims=True)
    xc = x - mu
    var = jnp.mean(xc * xc, axis=-1, keepdims=True)
    return xc * lax.rsqrt(var + eps) * g + b


def _split_bf16(x, terms):
    parts = []
    rest = x
    for _ in range(terms):
        part = rest.astype(BF16).astype(F32)
        parts.append(part)
        rest = rest - part
    return parts


def _bdot(a, b, kind):
    dims = {"nn": NN_DIMS, "nt": NT_DIMS, "tn": TN_DIMS}[kind]
    return lax.dot_general(a.astype(BF16), b.astype(BF16), dims, preferred_element_type=F32)


def _mask_mm(mask, x):
    return sum(jnp.dot(mask, part.astype(BF16), preferred_element_type=F32) for part in _split_bf16(x, 2))


def _mm_mask(x, mask):
    return sum(jnp.dot(part.astype(BF16), mask, preferred_element_type=F32) for part in _split_bf16(x, 2))


def _chunk_tril_mask(ts):
    row = lax.broadcasted_iota(jnp.int32, (ts, ts), 0)
    col = lax.broadcasted_iota(jnp.int32, (ts, ts), 1)
    shift = CHUNK.bit_length() - 1
    same = lax.shift_right_logical(row, shift) == lax.shift_right_logical(col, shift)
    return jnp.logical_and(same, col <= row).astype(BF16)


def _segment_ones_mask(width, seg):
    row = lax.broadcasted_iota(jnp.int32, (width, width), 0)
    col = lax.broadcasted_iota(jnp.int32, (width, width), 1)
    shift = seg.bit_length() - 1
    return (lax.shift_right_logical(row, shift) == lax.shift_right_logical(col, shift)).astype(BF16)


def _mm_kernel(x_ref, w_ref, o_ref):
    o_ref[...] = jnp.dot(x_ref[...], w_ref[...], preferred_element_type=F32).astype(o_ref.dtype)


def _matmul(x, w, *, out_dtype, tm, tn):
    m, k = x.shape
    n = w.shape[1]
    return pl.pallas_call(
        _mm_kernel,
        grid=(m // tm, n // tn),
        in_specs=[pl.BlockSpec((tm, k), lambda i, j: (i, 0)),
                  pl.BlockSpec((k, tn), lambda i, j: (0, j))],
        out_specs=pl.BlockSpec((tm, tn), lambda i, j: (i, j)),
        out_shape=jax.ShapeDtypeStruct((m, n), out_dtype),
        compiler_params=_params("parallel", "arbitrary"),
        name="matmul",
    )(x, w)


def _gated_mm_body(x1_ref, w1_ref, b1_ref, x2_refs, w2_ref, o_ref, *, nb, act):
    a = x1_ref[...]
    acc = None
    for n in range(nb):
        z = jnp.dot(a, w1_ref[n], preferred_element_type=F32)
        if b1_ref is not None:
            z = z + b1_ref[n]
        lhs2 = a if x2_refs is None else x2_refs[n][...]
        lin = jnp.dot(lhs2, w2_ref[n], preferred_element_type=F32)
        t = _act(z, act) * lin
        acc = t if acc is None else acc + t
    o_ref[...] = acc.astype(o_ref.dtype)


def _gated_mm_kernel(*refs, nb, act, has_bias, n_x2):
    refs = list(refs)
    x1_ref, w1_ref = refs[0], refs[1]
    pos = 2
    b1_ref = None
    if has_bias:
        b1_ref = refs[pos]
        pos += 1
    x2_refs = refs[pos:pos + n_x2] if n_x2 else None
    pos += n_x2
    w2_ref, o_ref = refs[pos], refs[pos + 1]
    _gated_mm_body(x1_ref, w1_ref, b1_ref, x2_refs, w2_ref, o_ref, nb=nb, act=act)


def _gated_mm(x1, w1, w2, *, x2s=None, b1=None, act, out_dtype, tm, tn):
    m, k1 = x1.shape
    nb, _, n = w1.shape
    k2 = w2.shape[1]
    in_specs = [pl.BlockSpec((tm, k1), lambda i, j: (i, 0)),
                pl.BlockSpec((nb, k1, tn), lambda i, j: (0, 0, j))]
    args = [x1, w1]
    if b1 is not None:
        in_specs.append(pl.BlockSpec((nb, 1, tn), lambda i, j: (0, 0, j)))
        args.append(b1)
    n_x2 = 0
    if x2s is not None:
        n_x2 = len(x2s)
        for x2 in x2s:
            in_specs.append(pl.BlockSpec((tm, k2), lambda i, j: (i, 0)))
            args.append(x2)
    in_specs.append(pl.BlockSpec((nb, k2, tn), lambda i, j: (0, 0, j)))
    args.append(w2)
    return pl.pallas_call(
        functools.partial(_gated_mm_kernel, nb=nb, act=act, has_bias=b1 is not None, n_x2=n_x2),
        grid=(m // tm, n // tn),
        in_specs=in_specs,
        out_specs=pl.BlockSpec((tm, tn), lambda i, j: (i, j)),
        out_shape=jax.ShapeDtypeStruct((m, n), out_dtype),
        compiler_params=_params("parallel", "arbitrary"),
        name="gated_matmul",
    )(*args)


def _grouped_gated_kernel(te_ref, nt_ref, x_ref, w1_ref, w2_ref, o_ref, w1_bf, w2_bf, *, act):
    i = pl.program_id(1)
    new_expert = jnp.logical_or(i == 0, te_ref[i] != te_ref[jnp.maximum(i - 1, 0)])

    @pl.when(new_expert)
    def _():
        w1_bf[...] = w1_ref[0].astype(BF16)
        w2_bf[...] = w2_ref[0].astype(BF16)

    @pl.when(i < nt_ref[0])
    def _():
        a = x_ref[...]
        z = jnp.dot(a, w1_bf[...], preferred_element_type=F32)
        lin = jnp.dot(a, w2_bf[...], preferred_element_type=F32)
        o_ref[...] = (_act(z, act) * lin).astype(o_ref.dtype)

    @pl.when(i >= nt_ref[0])
    def _():
        o_ref[...] = jnp.zeros_like(o_ref)


def _grouped_gated_mm(x, w1, w2, tile_expert, n_tiles, *, act, out_dtype, tm, tn):
    m, k = x.shape
    n = w1.shape[2]
    grid_spec = pltpu.PrefetchScalarGridSpec(
        num_scalar_prefetch=2,
        grid=(n // tn, m // tm),
        in_specs=[pl.BlockSpec((tm, k), lambda j, i, te, nt: (i, 0)),
                  pl.BlockSpec((1, k, tn), lambda j, i, te, nt: (te[i], 0, j)),
                  pl.BlockSpec((1, k, tn), lambda j, i, te, nt: (te[i], 0, j))],
        out_specs=pl.BlockSpec((tm, tn), lambda j, i, te, nt: (i, j)),
        scratch_shapes=[pltpu.VMEM((k, tn), BF16), pltpu.VMEM((k, tn), BF16)],
    )
    vmem_bytes = (2 * 2 * k * tn * 4 + 2 * k * tn * 2 + 2 * tm * k * 2
                  + 2 * tm * tn * jnp.dtype(out_dtype).itemsize + 2 * tm * tn * 4)
    return pl.pallas_call(
        functools.partial(_grouped_gated_kernel, act=act),
        grid_spec=grid_spec,
        out_shape=jax.ShapeDtypeStruct((m, n), out_dtype),
        compiler_params=_params("arbitrary", "arbitrary",
                                vmem_limit_bytes=max(V7X_VMEM_LIMIT_BYTES, vmem_bytes * 9 // 8)),
        name="grouped_gated_matmul",
    )(tile_expert, n_tiles, x, w1, w2)


def _grouped_mm_kernel(te_ref, nt_ref, x_ref, w_ref, o_ref):
    i = pl.program_id(0)

    @pl.when(i < nt_ref[0])
    def _():
        o_ref[...] = jnp.dot(x_ref[...], w_ref[0], preferred_element_type=F32)

    @pl.when(i >= nt_ref[0])
    def _():
        o_ref[...] = jnp.zeros_like(o_ref)


def _grouped_matmul(x, w, tile_expert, n_tiles, *, tm, tn):
    m, k = x.shape
    n = w.shape[2]
    grid_spec = pltpu.PrefetchScalarGridSpec(
        num_scalar_prefetch=2,
        grid=(m // tm, n // tn),
        in_specs=[pl.BlockSpec((tm, k), lambda i, j, te, nt: (i, 0)),
                  pl.BlockSpec((1, k, tn), lambda i, j, te, nt: (te[i], 0, j))],
        out_specs=pl.BlockSpec((tm, tn), lambda i, j, te, nt: (i, j)),
    )
    return pl.pallas_call(
        _grouped_mm_kernel,
        grid_spec=grid_spec,
        out_shape=jax.ShapeDtypeStruct((m, n), F32),
        compiler_params=_params("arbitrary", "arbitrary"),
        name="grouped_matmul",
    )(tile_expert, n_tiles, x, w)


def _proj_res_ln_kernel(h_ref, m_ref, w_ref, g_ref, b_ref, o_ref, obf_ref, *, alpha):
    y = alpha * h_ref[...] + jnp.dot(m_ref[...], w_ref[...], preferred_element_type=F32)
    out = _layer_norm(y, g_ref[...], b_ref[...], LN_EPS)
    o_ref[...] = out
    obf_ref[...] = out.astype(BF16)


def _proj_res_ln(h, m_in, w, g, b, *, alpha, tm):
    n, d = h.shape
    k = m_in.shape[1]
    return pl.pallas_call(
        functools.partial(_proj_res_ln_kernel, alpha=alpha),
        grid=(n // tm,),
        in_specs=[pl.BlockSpec((tm, d), lambda i: (i, 0)),
                  pl.BlockSpec((tm, k), lambda i: (i, 0)),
                  pl.BlockSpec((k, d), lambda i: (0, 0)),
                  pl.BlockSpec((1, d), lambda i: (0, 0)),
                  pl.BlockSpec((1, d), lambda i: (0, 0))],
        out_specs=[pl.BlockSpec((tm, d), lambda i: (i, 0)),
                   pl.BlockSpec((tm, d), lambda i: (i, 0))],
        out_shape=[jax.ShapeDtypeStruct((n, d), F32), jax.ShapeDtypeStruct((n, d), BF16)],
        compiler_params=_params("parallel"),
        name="proj_res_ln",
    )(h, m_in, w, g, b)


def _add_ln_kernel(*refs, alpha, n_terms, weighted):
    h_ref, hbf_ref, p_ref = refs[0:3]
    a_refs = refs[3:3 + n_terms]
    pos = 3 + n_terms
    c_ref = None
    if weighted:
        c_ref = refs[pos]
        pos += 1
    wg_ref, bg_ref, wp_ref, g_ref, b_ref, o_ref, obf_ref = refs[pos:pos + 7]
    gate = _sigmoid(jnp.dot(hbf_ref[...], wg_ref[...], preferred_element_type=F32) + bg_ref[...])
    ple = gate * jnp.dot(p_ref[...], wp_ref[...], preferred_element_type=F32)
    y = alpha * h_ref[...] + ple
    for j in range(n_terms):
        t = a_refs[j][...]
        if weighted:
            t = t * c_ref[:, j:j + 1]
        y = y + t
    out = _layer_norm(y, g_ref[...], b_ref[...], LN_EPS)
    o_ref[...] = out
    obf_ref[...] = out.astype(BF16)


def _add_ln(h, h_bf, p_bf, terms, coef, ple_wg, ple_bg, ple_wp, g, b, *, alpha, tm):
    n, d = h.shape
    dp = p_bf.shape[1]
    row = pl.BlockSpec((tm, d), lambda i: (i, 0))
    vec = pl.BlockSpec((1, d), lambda i: (0, 0))
    in_specs = [row, row, pl.BlockSpec((tm, dp), lambda i: (i, 0))] + [row] * len(terms)
    args = [h, h_bf, p_bf] + list(terms)
    if coef is not None:
        in_specs.append(pl.BlockSpec((tm, coef.shape[1]), lambda i: (i, 0)))
        args.append(coef)
    in_specs += [pl.BlockSpec((d, d), lambda i: (0, 0)), vec, pl.BlockSpec((dp, d), lambda i: (0, 0)), vec, vec]
    args += [ple_wg, ple_bg, ple_wp, g, b]
    return pl.pallas_call(
        functools.partial(_add_ln_kernel, alpha=alpha, n_terms=len(terms), weighted=coef is not None),
        grid=(n // tm,),
        in_specs=in_specs,
        out_specs=[row, row],
        out_shape=[jax.ShapeDtypeStruct((n, d), F32), jax.ShapeDtypeStruct((n, d), BF16)],
        compiler_params=_params("parallel"),
        name="add_ln",
    )(*args)


def _conv_kernel(u_ref, dw_ref, dwb_ref, g_ref, b_ref, o_ref, buf, shifted, *, ts, rb, width):
    s = pl.program_id(1)

    @pl.when(s == 0)
    def _():
        buf[0:CONV_HALO, :] = jnp.zeros((CONV_HALO, buf.shape[1]), F32)

    buf[CONV_HALO:CONV_HALO + ts, :] = u_ref[0]
    lead = CONV_HALO - (width - 1)
    span = shifted.shape[1]
    for r in range(1, SUBLANES):
        shifted[r - 1] = buf[r:r + span, :]
    for r0 in range(0, ts, rb):
        acc = jnp.broadcast_to(dwb_ref[...], (rb, buf.shape[1]))
        for j in range(width):
            m, r = divmod(lead + j, SUBLANES)
            lo = r0 + SUBLANES * m
            rows = buf[lo:lo + rb, :] if r == 0 else shifted[r - 1, lo:lo + rb, :]
            acc = acc + rows * dw_ref[SUBLANES * j:SUBLANES * (j + 1), :]
        y = _silu(_layer_norm(acc, g_ref[...], b_ref[...], LN_EPS))
        o_ref[0, r0:r0 + rb, :] = y.astype(o_ref.dtype)
    buf[0:CONV_HALO, :] = buf[ts:ts + CONV_HALO, :]


def _conv_module(u, dw, dw_b, ln_g, ln_b, *, ts, rb):
    bsz, seq, c = u.shape
    width = dw.shape[0]
    assert rb == SUBLANES
    dw_rep = jnp.repeat(dw.astype(F32), SUBLANES, axis=0)
    vec = pl.BlockSpec((1, c), lambda b, s: (0, 0))
    return pl.pallas_call(
        functools.partial(_conv_kernel, ts=ts, rb=rb, width=width),
        grid=(bsz, seq // ts),
        in_specs=[pl.BlockSpec((1, ts, c), lambda b, s: (b, s, 0)),
                  pl.BlockSpec((width * SUBLANES, c), lambda b, s: (0, 0)),
                  vec, vec, vec],
        out_specs=pl.BlockSpec((1, ts, c), lambda b, s: (b, s, 0)),
        out_shape=jax.ShapeDtypeStruct((bsz, seq, c), BF16),
        scratch_shapes=[pltpu.VMEM((CONV_HALO + ts, c), F32),
                        pltpu.VMEM((SUBLANES - 1, CONV_HALO + ts - SUBLANES, c), F32)],
        compiler_params=_params("parallel", "arbitrary"),
        name="conv_module",
    )(u, dw_rep, dw_b.reshape(1, c), ln_g.reshape(1, c), ln_b.reshape(1, c))


def _hgrn_steps(zq, zf, zi, zg, lb, ng, state_t, k_buf, b_buf, *, ts):
    nc = ts // CHUNK
    nsub = CHUNK // SUB
    q = _silu(zq) * (HGRN_EXPAND ** -0.5)
    k = (1.0 - lb) * _sigmoid(-zf)
    v = zi
    b = _mask_mm(_chunk_tril_mask(ts), jnp.log(1.0 - k))
    k_buf[...] = k
    b_buf[...] = b

    row = lax.broadcasted_iota(jnp.int32, (CHUNK, CHUNK), 0)
    col = lax.broadcasted_iota(jnp.int32, (CHUNK, CHUNK), 1)
    row_sub = lax.shift_right_logical(row, SUB.bit_length() - 1)
    col_sub = lax.shift_right_logical(col, SUB.bit_length() - 1)
    earlier_sub = col_sub < row_sub
    same_sub_causal = jnp.logical_and(col_sub == row_sub, col <= row)
    strip_col = lax.broadcasted_iota(jnp.int32, (SUB, CHUNK), 1)
    yield

    o_intra, upd, decay, q_dec = [], [], [], []
    for c in range(nc):
        base = c * CHUNK
        q_c, k_c, b_c, v_c = (t[base:base + CHUNK] for t in (q, k, b, v))
        off_strips, diag_strips = [], []
        for i in range(nsub):
            lo = base + i * SUB
            q_i, b_i = q[lo:lo + SUB], b[lo:lo + SUB]
            if i == 0:
                off_strips.append(jnp.zeros((SUB, CHUNK), F32))
            else:
                beta = b_buf[lo - 1:lo, :]
                q_hat = q_i * jnp.exp(b_i - beta)
                k_hat = k_c * jnp.exp(jnp.minimum(beta - b_c, 0.0))
                off_strips.append(_bdot(q_hat, k_hat, "nt"))
            strip = jnp.zeros((SUB, CHUNK), F32)
            for j in range(SUB):
                sj = i * SUB + j
                k_row = k_buf[lo + j:lo + j + 1, :]
                b_row = b_buf[lo + j:lo + j + 1, :]
                e = jnp.exp(jnp.minimum(b_i - b_row, 0.0))
                colv = jnp.sum(q_i * (k_row * e), axis=-1, keepdims=True)
                strip = jnp.where(strip_col == sj, colv, strip)
            diag_strips.append(strip)
            yield
        att =(jnp.where(earlier_sub, jnp.concatenate(off_strips, axis=0), 0.0)
               + jnp.where(same_sub_causal, jnp.concatenate(diag_strips, axis=0), 0.0))
        o_intra.append(_bdot(att, v_c, "nn"))
        b_last = b_buf[base + CHUNK - 1:base + CHUNK, :]
        upd.append(_bdot(v_c, k_c * jnp.exp(b_last - b_c), "tn"))
        decay.append(jnp.exp(b_last))
        q_dec.append(q_c * jnp.exp(b_c))
        yield

    st = state_t[...]
    outs = []
    for c in range(nc):
        outs.append(o_intra[c] + _bdot(q_dec[c], st, "nt"))
        st = st * decay[c] + upd[c]
    state_t[...] = st
    yield
    o = jnp.concatenate(outs, axis=0)
    o = o * lax.rsqrt(jnp.mean(o * o, axis=-1, keepdims=True) + LN_EPS) * ng
    return o * _silu(zg)


def _token_shift(z, prev_row, mu):
    rows = lax.broadcasted_iota(jnp.int32, z.shape, 0)
    shifted = jnp.where(rows == 0, prev_row, pltpu.roll(z, 1, axis=0))
    return z + (shifted - z) * mu


def _rwkv_prep_kernel(*refs, vres, c):
    (zrkv_ref, zl_ref, mu_rkv_ref, mu_l_ref, w0_ref, w2_ref, a0_ref, a2_ref, g2_ref,
     kk_ref, ka_ref) = refs[:11]
    pos = 11
    if vres:
        v0_ref, v2_ref, vfirst_ref = refs[pos:pos + 3]
        pos += 3
    r_o, k_o, v_o, ld_o, kk_o, a_o, g_o, prev_rkv, prev_l = refs[pos:pos + 9]
    s = pl.program_id(1)

    @pl.when(s == 0)
    def _():
        prev_rkv[...] = jnp.zeros_like(prev_rkv)
        prev_l[...] = jnp.zeros_like(prev_l)

    z = zrkv_ref[0]
    zl = zl_ref[0]
    ts = z.shape[0]
    z_mix = _token_shift(z, prev_rkv[...], mu_rkv_ref[...])
    zl_mix = _token_shift(zl, prev_l[...], mu_l_ref[...])
    prev_rkv[...] = z[ts - 1:ts, :]
    prev_l[...] = zl[ts - 1:ts, :]

    zr, zk, zv = z_mix[:, 0:c], z_mix[:, c:2 * c], z_mix[:, 2 * c:3 * c]
    zw, za, zg = zl_mix[:, 0:128], zl_mix[:, 128:256], zl_mix[:, 256:512]

    def lora(act_in, w_ref):
        return jnp.dot(act_in.astype(BF16), w_ref[...], preferred_element_type=F32)

    x = -(w0_ref[...] + lora(jnp.tanh(zw), w2_ref))
    softplus = jnp.maximum(x, 0.0) + jnp.log(1.0 + jnp.exp(-jnp.abs(x)))
    w = -softplus - 0.5
    a = _sigmoid(a0_ref[...] + lora(za, a2_ref))
    g = lora(_sigmoid(zg), g2_ref)
    v = zv
    if vres:
        zm = zl_mix[:, 512:640]
        v = zv + (vfirst_ref[0] - zv) * _sigmoid(v0_ref[...] + lora(zm, v2_ref))
    r_o[0] = zr
    k_o[0] = zk * (1.0 + (a - 1.0) * ka_ref[...])
    v_o[0] = v
    ld_o[0] = -jnp.exp(w)
    kk_o[0] = zk * kk_ref[...]
    a_o[0] = a
    g_o[0] = g


def _rwkv_prep(z_rkv, z_lora, mu_rkv, mu_l, w0, w2, a0, a2, g2, k_k, k_a, v0, v2, v_first, *, ts):
    bsz, seq, c3 = z_rkv.shape
    c = c3 // 3
    lw = z_lora.shape[2]
    vres = v0 is not None

    def full(arr):
        return pl.BlockSpec(arr.shape, lambda b, s: (0,) * arr.ndim)

    tile = pl.BlockSpec((1, ts, c), lambda b, s: (b, s, 0))
    args = [z_rkv, z_lora, mu_rkv, mu_l, w0, w2, a0, a2, g2, k_k, k_a]
    in_specs = [pl.BlockSpec((1, ts, c3), lambda b, s: (b, s, 0)),
                pl.BlockSpec((1, ts, lw), lambda b, s: (b, s, 0))] + [full(a) for a in args[2:]]
    if vres:
        args += [v0, v2, v_first]
        in_specs += [full(v0), full(v2), tile]
    return pl.pallas_call(
        functools.partial(_rwkv_prep_kernel, vres=vres, c=c),
        grid=(bsz, seq // ts),
        in_specs=in_specs,
        out_specs=[tile] * 7,
        out_shape=[jax.ShapeDtypeStruct((bsz, seq, c), F32)] * 7,
        scratch_shapes=[pltpu.VMEM((1, c3), F32), pltpu.VMEM((1, lw), F32)],
        compiler_params=_params("parallel", "arbitrary"),
        name="rwkv_prep",
    )(*args)


def _rwkv_steps(r, k, v, ld, kk, a, g, rk, lng, lnb, state, *, ts, hg):
    n = RWKV_HEAD
    nc = ts // CHUNK
    width = hg * n
    head_ones = _segment_ones_mask(width, n)
    lp = _mask_mm(_chunk_tril_mask(ts), ld)
    norm = jnp.sqrt(_mm_mask(kk * kk, head_ones))
    kkn = kk / jnp.maximum(norm, 1e-12)
    bvec = kkn * a
    inv_p = jnp.exp(-lp)
    a_t = -kkn * jnp.exp(lp - ld)
    r_t = r * jnp.exp(lp)
    b_t = bvec * inv_p
    k_t = k * inv_p

    row = lax.broadcasted_iota(jnp.int32, (CHUNK, CHUNK), 0)
    col = lax.broadcasted_iota(jnp.int32, (CHUNK, CHUNK), 1)
    incl = row >= col
    strict = row > col
    eye = (row == col).astype(F32)

    cells = [(c, hh) for c in range(nc) for hh in range(hg)]

    def blk(x, c, hh):
        return x[c * CHUNK:(c + 1) * CHUNK, hh * n:(hh + 1) * n]

    p_last = [jnp.exp(lp[(c + 1) * CHUNK - 1:(c + 1) * CHUNK, :]) for c in range(nc)]
    to_end = jnp.concatenate(
        [jnp.exp(lp[(c + 1) * CHUNK - 1:(c + 1) * CHUNK, :] - lp[c * CHUNK:(c + 1) * CHUNK, :]) for c in range(nc)],
        axis=0)
    b_h = bvec * to_end
    k_h = k * to_end
    yield

    def stage(fn):
        out = {}
        for idx, cell in enumerate(cells):
            out[cell] = fn(cell)
            if idx % CELLS_PER_YIELD == CELLS_PER_YIELD - 1:
                yield
        return out

    a_c = {cell: blk(a_t, *cell) for cell in cells}
    v_c = {cell: blk(v, *cell) for cell in cells}
    gram = yield from stage(lambda cell: _bdot(
        jnp.concatenate([a_c[cell], blk(r_t, *cell)], axis=0),
        jnp.concatenate([blk(b_t, *cell), blk(k_t, *cell)], axis=0), "nt"))
    t_ab = {cell: jnp.where(strict, gram[cell][0:CHUNK, 0:CHUNK], 0.0) for cell in cells}
    t_ak = {cell: jnp.where(strict, gram[cell][0:CHUNK, CHUNK:], 0.0) for cell in cells}
    t_rb = {cell: jnp.where(incl, gram[cell][CHUNK:, 0:CHUNK], 0.0) for cell in cells}
    t_rk = {cell: jnp.where(incl, gram[cell][CHUNK:, CHUNK:], 0.0) for cell in cells}

    power = yield from stage(lambda cell: _bdot(t_ab[cell], t_ab[cell], "nn"))
    inv = {cell: eye + t_ab[cell] for cell in cells}
    for _ in range(CHUNK.bit_length() - 3):
        both = yield from stage(lambda cell: _bdot(
            jnp.concatenate([power[cell], inv[cell]], axis=0), power[cell], "nn"))
        power = {cell: both[cell][0:CHUNK] for cell in cells}
        inv = {cell: inv[cell] + both[cell][CHUNK:] for cell in cells}
    last = yield from stage(lambda cell: _bdot(inv[cell], power[cell], "nn"))
    inv = {cell: inv[cell] + last[cell] for cell in cells}

    tv = yield from stage(lambda cell: _bdot(
        jnp.concatenate([t_ak[cell], t_rk[cell]], axis=0), v_c[cell], "nn"))
    wu = yield from stage(lambda cell: _bdot(
        inv[cell], jnp.concatenate([a_c[cell], tv[cell][0:CHUNK]], axis=1), "nn"))
    qy = yield from stage(lambda cell: _bdot(t_rb[cell], wu[cell], "nn"))
    gh = yield from stage(lambda cell: _bdot(blk(b_h, *cell), wu[cell], "tn"))
    kv = yield from stage(lambda cell: _bdot(blk(k_h, *cell), v_c[cell], "tn"))
    q_g, y0, h_t = {}, {}, {}
    for cell in cells:
        c, hh = cell
        q_eff = blk(r_t, *cell) + qy[cell][:, 0:n]
        g_t = eye * p_last[c][:, hh * n:(hh + 1) * n] + gh[cell][:, 0:n]
        q_g[cell] = jnp.concatenate([q_eff, g_t], axis=0)
        y0[cell] = qy[cell][:, n:] + tv[cell][CHUNK:]
        h_t[cell] = gh[cell][:, n:] + kv[cell]
    yield

    z = [state[hh] for hh in range(hg)]
    y_rows = []
    for c in range(nc):
        y_heads = []
        for hh in range(hg):
            res = _bdot(q_g[(c, hh)], z[hh], "nn")
            y_heads.append(res[0:CHUNK] + y0[(c, hh)])
            z[hh] = res[CHUNK:] + h_t[(c, hh)]
        y_rows.append(jnp.concatenate(y_heads, axis=1))
        yield
    for hh in range(hg):
        state[hh] = z[hh]
    y = jnp.concatenate(y_rows, axis=0)

    inv_n = 1.0 / n
    mu = _mm_mask(y, head_ones) * inv_n
    yc = y - mu
    var = _mm_mask(yc * yc, head_ones) * inv_n
    yn = yc * lax.rsqrt(var + RWKV_LN_EPS) * lng + lnb
    bonus = _mm_mask(r * k * rk, head_ones) * v
    return (yn + bonus) * g


def _round_robin(gens):
    results = [None] * len(gens)
    live = list(range(len(gens)))
    while live:
        for idx in list(live):
            try:
                next(gens[idx])
            except StopIteration as stop:
                results[idx] = stop.value
                live.remove(idx)
    return results


def _dual_scan_kernel(r_ref, k_ref, v_ref, ld_ref, kk_ref, a_ref, g_ref, rk_ref, lng_ref, lnb_ref,
                      zq_ref, zf_ref, zi_ref, zg_ref, lb_ref, ng_ref,
                      o_rwkv_ref, o_hgrn_ref, r_state, h_state, k_buf, b_buf, *, ts, hg, hh):
    s = pl.program_id(2)

    @pl.when(s == 0)
    def _():
        r_state[...] = jnp.zeros_like(r_state)
        h_state[...] = jnp.zeros_like(h_state)

    dh = HGRN_EXPAND
    gens = [_rwkv_steps(r_ref[0], k_ref[0], v_ref[0], ld_ref[0], kk_ref[0], a_ref[0], g_ref[0],
                        rk_ref[...], lng_ref[...], lnb_ref[...], r_state, ts=ts, hg=hg)]
    for j in range(hh):
        cs = slice(j * dh, (j + 1) * dh)
        gens.append(_hgrn_steps(zq_ref[0, :, cs], zf_ref[0, :, cs], zi_ref[0, :, cs], zg_ref[0, :, cs],
                                lb_ref[:, cs], ng_ref[...], h_state.at[j], k_buf.at[j], b_buf.at[j], ts=ts))
    outs = _round_robin(gens)
    o_rwkv_ref[0] = outs[0].astype(o_rwkv_ref.dtype)
    o_hgrn_ref[0] = jnp.concatenate(outs[1:], axis=1).astype(o_hgrn_ref.dtype)


def _dual_scan(r, k, v, ld, kk, a, g, r_k, lnx_g, lnx_b, z_hgrn, lb, norm_g, *, ts, groups):
    bsz, seq, c = r.shape
    wr = c // groups
    hg = wr // RWKV_HEAD
    dh = HGRN_EXPAND
    ch = z_hgrn.shape[2] // 4
    wh = ch // groups
    hh = wh // dh
    rtile = pl.BlockSpec((1, ts, wr), lambda b, h, s: (b, s, h))
    rvec = pl.BlockSpec((1, wr), lambda b, h, s: (0, h))
    htile = pl.BlockSpec((1, ts, wh), lambda b, h, s: (b, s, h))

    def zspec(group):
        return pl.BlockSpec((1, ts, wh), lambda b, h, s: (b, s, group * groups + h))

    return pl.pallas_call(
        functools.partial(_dual_scan_kernel, ts=ts, hg=hg, hh=hh),
        grid=(bsz, groups, seq // ts),
        in_specs=[rtile] * 7 + [rvec] * 3 + [zspec(0), zspec(1), zspec(2), zspec(3),
                                             pl.BlockSpec((1, wh), lambda b, h, s: (0, h)),
                                             pl.BlockSpec((1, dh), lambda b, h, s: (0, 0))],
        out_specs=[rtile, htile],
        out_shape=[jax.ShapeDtypeStruct((bsz, seq, c), BF16), jax.ShapeDtypeStruct((bsz, seq, ch), BF16)],
        scratch_shapes=[pltpu.VMEM((hg, RWKV_HEAD, RWKV_HEAD), F32),
                        pltpu.VMEM((hh, dh, dh), F32),
                        pltpu.VMEM((hh, ts, dh), F32),
                        pltpu.VMEM((hh, ts, dh), F32)],
        compiler_params=_params("parallel", "parallel", "arbitrary"),
        name="dual_scan",
    )(r, k, v, ld, kk, a, g, r_k.reshape(1, c), lnx_g.reshape(1, c), lnx_b.reshape(1, c),
      z_hgrn, z_hgrn, z_hgrn, z_hgrn, lb.reshape(1, ch), norm_g.reshape(1, dh))


def _router_kernel(h_ref, w_ref, idx_ref, wt_ref, *, n_experts):
    logits = jnp.dot(h_ref[...], w_ref[...], precision=HI, preferred_element_type=F32)
    lane = lax.broadcasted_iota(jnp.int32, logits.shape, 1)
    neg = jnp.float32(-jnp.inf)
    big = jnp.int32(logits.shape[1])
    x = jnp.where(lane < n_experts, logits, neg)
    m1 = jnp.max(x, axis=-1, keepdims=True)
    i1 = jnp.min(jnp.where(x == m1, lane, big), axis=-1, keepdims=True)
    x2 = jnp.where(lane == i1, neg, x)
    m2 = jnp.max(x2, axis=-1, keepdims=True)
    i2 = jnp.min(jnp.where(x2 == m2, lane, big), axis=-1, keepdims=True)
    e2 = jnp.exp(m2 - m1)
    w1 = 1.0 / (1.0 + e2)
    w2 = e2 / (1.0 + e2)
    idx_ref[...] = jnp.where(lane == 0, i1, jnp.where(lane == 1, i2, 0))
    wt_ref[...] = jnp.where(lane == 0, w1, jnp.where(lane == 1, w2, 0.0))


def _router(h, w_router, *, tm):
    n, d = h.shape
    n_experts = w_router.shape[1]
    w_pad = jnp.zeros((d, 128), F32).at[:, :n_experts].set(w_router)
    out = pl.BlockSpec((tm, 128), lambda i: (i, 0))
    return pl.pallas_call(
        functools.partial(_router_kernel, n_experts=n_experts),
        grid=(n // tm,),
        in_specs=[pl.BlockSpec((tm, d), lambda i: (i, 0)),
                  pl.BlockSpec((d, 128), lambda i: (0, 0))],
        out_specs=[out, out],
        out_shape=[jax.ShapeDtypeStruct((n, 128), jnp.int32), jax.ShapeDtypeStruct((n, 128), F32)],
        compiler_params=_params("parallel"),
        name="moe_router",
    )(h, w_pad)


def _moe(h, h_bf, w_router, wg, wu, wd, *, tm):
    n, d = h.shape
    n_experts = wg.shape[0]
    idx_pad, wt_pad = _router(h, w_router, tm=512)
    e_flat = idx_pad[:, :TOP_K].reshape(-1)
    onehot = (e_flat[:, None] == jnp.arange(n_experts, dtype=jnp.int32)[None, :]).astype(jnp.int32)
    csum = jnp.cumsum(onehot, axis=0)
    rank = jnp.sum(csum * onehot, axis=1) - 1
    counts = csum[-1]
    padded = ((counts + tm - 1) // tm) * tm
    ends = jnp.cumsum(padded)
    starts = ends - padded
    pos = starts[e_flat] + rank
    n_rows = TOP_K * n + n_experts * tm
    src = jnp.zeros((n_rows,), jnp.int32).at[pos].set(jnp.arange(TOP_K * n, dtype=jnp.int32) // TOP_K)
    n_tiles = (ends[-1] // tm).astype(jnp.int32).reshape(1)
    tile_start = jnp.arange(n_rows // tm, dtype=jnp.int32) * tm
    tile_expert = jnp.minimum(jnp.searchsorted(ends, tile_start, side="right"), n_experts - 1).astype(jnp.int32)
    x_sorted = h_bf.at[src].get(mode="promise_in_bounds")
    h1 = _grouped_gated_mm(x_sorted, wg, wu, tile_expert, n_tiles, act="silu", out_dtype=BF16, tm=tm, tn=1024)
    y_sorted = _grouped_matmul(h1, wd, tile_expert, n_tiles, tm=tm, tn=512)
    pos2 = pos.reshape(n, TOP_K)
    y0 = y_sorted.at[pos2[:, 0]].get(mode="promise_in_bounds")
    y1 = y_sorted.at[pos2[:, 1]].get(mode="promise_in_bounds")
    return [y0, y1], wt_pad


def _pad_cols(w, width):
    return jnp.pad(w, ((0, 0), (0, width - w.shape[1])))


def _pad_rows(w, height):
    return jnp.pad(w, ((0, height - w.shape[0]), (0, 0)))


def kernel(x, p, w_in, w_in_vres, conv_dw, conv_dw_b, conv_ln_g, conv_ln_b, rwkv_mu, rwkv_mu_vres, rwkv_w0, rwkv_w2, rwkv_a0, rwkv_a2, rwkv_v0, rwkv_v2, rwkv_g2, rwkv_kk, rwkv_ka, rwkv_rk, rwkv_lnx_g, rwkv_lnx_b, hgrn_lb, hgrn_norm_g, w_branch, b_gate, w_out, ln1_g, ln1_b, ffn_wg, ffn_wu, ffn_wd, moe_router, moe_wg, moe_wu, moe_wd, ple_wp, ple_wg, ple_bg, ln2_g, ln2_b):
    bsz, seq, d = x.shape
    depth = w_in.shape[0]
    n = bsz * seq
    cc = conv_dw.shape[2]
    cr = rwkv_w0.shape[1]
    ch = hgrn_lb.shape[1]
    d_decay, d_aaa, d_gate, d_mv = rwkv_w2.shape[1], rwkv_a2.shape[1], rwkv_g2.shape[1], rwkv_v2.shape[1]
    n_branch = w_branch.shape[1]
    alpha = float((2 * depth) ** 0.25)
    tm = min(1024, n)
    ts = min(256, seq)

    lb_soft = jax.nn.softmax(hgrn_lb.astype(F32), axis=0)
    lower_bounds = jnp.cumsum(lb_soft, axis=0) - lb_soft[0]

    h = x.reshape(n, d)
    h_bf = h.astype(BF16)
    v_first = None
    for i in range(depth):
        vres = i > 0
        wi = w_in[i]
        o_rwkv = 2 * cc
        o_lora = o_rwkv + 3 * cr
        o_hgrn = o_lora + d_decay + d_aaa + d_gate
        o_gate = o_hgrn + 4 * ch

        w_val = wi[:, 0:cc].astype(BF16)[None]
        w_glu = wi[:, cc:2 * cc].astype(BF16)[None]
        u = _gated_mm(h_bf, w_glu, w_val, act="sigmoid", out_dtype=F32, tm=tm, tn=min(1024, cc))
        y_conv = _conv_module(u.reshape(bsz, seq, cc), conv_dw[i], conv_dw_b[i], conv_ln_g[i], conv_ln_b[i],
                              ts=ts, rb=SUBLANES).reshape(n, cc)

        z_rkv = _matmul(h_bf, wi[:, o_rwkv:o_lora].astype(BF16), out_dtype=F32, tm=tm, tn=min(1024, cr))
        lora_cols = [_pad_cols(wi[:, o_lora:o_lora + d_decay], 128),
                     _pad_cols(wi[:, o_lora + d_decay:o_lora + d_decay + d_aaa], 128),
                     _pad_cols(wi[:, o_lora + d_decay + d_aaa:o_hgrn], 256)]
        mu_i = rwkv_mu[i]
        mu_cols = [_pad_cols(mu_i[None, 3 * cr:3 * cr + d_decay], 128),
                   _pad_cols(mu_i[None, 3 * cr + d_decay:3 * cr + d_decay + d_aaa], 128),
                   _pad_cols(mu_i[None, 3 * cr + d_decay + d_aaa:], 256)]
        if vres:
            lora_cols.append(_pad_cols(w_in_vres[i - 1], 128))
            mu_cols.append(_pad_cols(rwkv_mu_vres[i - 1][None], 128))
        w_lora = jnp.concatenate(lora_cols, axis=1).astype(BF16)
        mu_l = jnp.concatenate(mu_cols, axis=1)
        z_lora = _matmul(h_bf, w_lora, out_dtype=F32, tm=tm, tn=w_lora.shape[1])
        r, k, v, ld, kk, a, g = _rwkv_prep(
            z_rkv.reshape(bsz, seq, 3 * cr), z_lora.reshape(bsz, seq, -1),
            mu_i[None, :3 * cr], mu_l,
            rwkv_w0[i][None], _pad_rows(rwkv_w2[i], 128).astype(BF16),
            rwkv_a0[i][None], _pad_rows(rwkv_a2[i], 128).astype(BF16),
            _pad_rows(rwkv_g2[i], 256).astype(BF16),
            rwkv_kk[i][None], rwkv_ka[i][None],
            rwkv_v0[i - 1][None] if vres else None,
            _pad_rows(rwkv_v2[i - 1], 128).astype(BF16) if vres else None,
            v_first, ts=ts)
        if not vres:
            v_first = v
        z_hgrn = _matmul(h_bf, wi[:, o_hgrn:o_gate].astype(BF16), out_dtype=F32, tm=tm, tn=min(1024, 4 * ch))
        y_rwkv, y_hgrn = _dual_scan(r, k, v, ld, kk, a, g, rwkv_rk[i], rwkv_lnx_g[i], rwkv_lnx_b[i],
                                    z_hgrn.reshape(bsz, seq, 4 * ch), lower_bounds[i], hgrn_norm_g[i],
                                    ts=ts, groups=SCAN_GROUPS)
        y_rwkv = y_rwkv.reshape(n, cr)
        y_hgrn = y_hgrn.reshape(n, ch)

        w_g = wi[:, o_gate:o_gate + n_branch * d].reshape(d, n_branch, d).transpose(1, 0, 2).astype(BF16)
        merged = _gated_mm(h_bf, w_g, w_branch[i].astype(BF16), x2s=[y_conv, y_rwkv, y_hgrn],
                           b1=b_gate[i][:, None, :], act="sigmoid", out_dtype=BF16, tm=min(512, n), tn=min(512, d))
        h, h_bf = _proj_res_ln(h, merged, w_out[i].astype(BF16), ln1_g[i][None], ln1_b[i][None],
                               alpha=alpha, tm=min(256, n))

        j = i // 2
        if i % 2 == 0:
            h1 = _gated_mm(h_bf, ffn_wg[j].astype(BF16)[None], ffn_wu[j].astype(BF16)[None],
                           act="silu", out_dtype=BF16, tm=tm, tn=1024)
            terms, coef = [_matmul(h1, ffn_wd[j].astype(BF16), out_dtype=F32, tm=min(512, n), tn=512)], None
        else:
            terms, coef = _moe(h, h_bf, moe_router[j], moe_wg[j], moe_wu[j],
                               moe_wd[j].astype(BF16), tm=min(512, n))
        h, h_bf = _add_ln(h, h_bf, p[i].reshape(n, -1).astype(BF16), terms, coef,
                          ple_wg[i].astype(BF16), ple_bg[i][None], ple_wp[i].astype(BF16),
                          ln2_g[i][None], ln2_b[i][None], alpha=alpha, tm=min(256, n))
    return h.reshape(bsz, seq, d)
```

```python
import functools

import jax
import jax.numpy as jnp
from jax import lax
from jax.experimental import pallas as pl
from jax.experimental.pallas import tpu as pltpu

F32 = jnp.float32
BF16 = jnp.bfloat16
HI = lax.Precision.HIGHEST

V7X_VMEM_LIMIT_BYTES = 48 * 1024 * 1024

LN_EPS = 1e-5
RWKV_LN_EPS = 64e-5
RWKV_HEAD = 64
HGRN_EXPAND = 128
CHUNK = 64
SUB = 16
CONV_HALO = 32
SUBLANES = 8
LOG2_E = 1.4426950408889634
TOP_K = 2
SCAN_GROUPS = 4
CELLS_PER_YIELD = 4

NN_DIMS = (((1,), (0,)), ((), ()))
NT_DIMS = (((1,), (1,)), ((), ()))
TN_DIMS = (((0,), (0,)), ((), ()))


def _params(*semantics, vmem_limit_bytes=V7X_VMEM_LIMIT_BYTES):
    return pltpu.CompilerParams(dimension_semantics=semantics, vmem_limit_bytes=vmem_limit_bytes)


def _sigmoid(z):
    return 1.0 / (1.0 + jnp.exp(-z))


def _silu(z):
    return z * _sigmoid(z)


def _act(z, kind):
    return _sigmoid(z) if kind == "sigmoid" else _silu(z)


def _layer_norm(x, g, b, eps):
    mu = jnp.mean(x, axis=-1, keepdims=True)
    xc = x - mu
    var = jnp.mean(xc * xc, axis=-1, keepdims=True)
    return xc * lax.rsqrt(var + eps) * g + b


def _split_bf16(x, terms):
    parts = []
    rest = x
    for _ in range(terms):
        part = rest.astype(BF16).astype(F32)
        parts.append(part)
        rest = rest - part
    return parts


def _bdot(a, b, kind):
    dims = {"nn": NN_DIMS, "nt": NT_DIMS, "tn": TN_DIMS}[kind]
    return lax.dot_general(a.astype(BF16), b.astype(BF16), dims, preferred_element_type=F32)


def _mask_mm(mask, x):
    return sum(jnp.dot(mask, part.astype(BF16), preferred_element_type=F32) for part in _split_bf16(x, 2))


def _mm_mask(x, mask):
    return sum(jnp.dot(part.astype(BF16), mask, preferred_element_type=F32) for part in _split_bf16(x, 2))


def _chunk_tril_mask(ts):
    row = lax.broadcasted_iota(jnp.int32, (ts, ts), 0)
    col = lax.broadcasted_iota(jnp.int32, (ts, ts), 1)
    shift = CHUNK.bit_length() - 1
    same = lax.shift_right_logical(row, shift) == lax.shift_right_logical(col, shift)
    return jnp.logical_and(same, col <= row).astype(BF16)


def _segment_ones_mask(width, seg):
    row = lax.broadcasted_iota(jnp.int32, (width, width), 0)
    col = lax.broadcasted_iota(jnp.int32, (width, width), 1)
    shift = seg.bit_length() - 1
    return (lax.shift_right_logical(row, shift) == lax.shift_right_logical(col, shift)).astype(BF16)


def _mm_kernel(x_ref, w_ref, o_ref):
    o_ref[...] = jnp.dot(x_ref[...], w_ref[...], preferred_element_type=F32).astype(o_ref.dtype)


def _mm_vmem_limit(tm, k, tn, x_dtype, w_dtype, out_dtype):
    blocks = (tm * k * jnp.dtype(x_dtype).itemsize + k * tn * jnp.dtype(w_dtype).itemsize
              + tm * tn * jnp.dtype(out_dtype).itemsize)
    return max(V7X_VMEM_LIMIT_BYTES, (2 * blocks + tm * tn * 4) * 9 // 8)


def _matmul(x, w, *, out_dtype, tm, tn):
    m, k = x.shape
    n = w.shape[1]
    return pl.pallas_call(
        _mm_kernel,
        grid=(m // tm, n // tn),
        in_specs=[pl.BlockSpec((tm, k), lambda i, j: (i, 0)),
                  pl.BlockSpec((k, tn), lambda i, j: (0, j))],
        out_specs=pl.BlockSpec((tm, tn), lambda i, j: (i, j)),
        out_shape=jax.ShapeDtypeStruct((m, n), out_dtype),
        compiler_params=_params("parallel", "arbitrary",
                                vmem_limit_bytes=_mm_vmem_limit(tm, k, tn, x.dtype, w.dtype, out_dtype)),
        name="matmul",
    )(x, w)


def _gated_mm_body(x1_ref, w1_ref, b1_ref, x2_refs, w2_ref, o_ref, *, nb, act):
    a = x1_ref[...]
    acc = None
    for n in range(nb):
        z = jnp.dot(a, w1_ref[n], preferred_element_type=F32)
        if b1_ref is not None:
            z = z + b1_ref[n]
        lhs2 = a if x2_refs is None else x2_refs[n][...]
        lin = jnp.dot(lhs2, w2_ref[n], preferred_element_type=F32)
        t = _act(z, act) * lin
        acc = t if acc is None else acc + t
    o_ref[...] = acc.astype(o_ref.dtype)


def _gated_mm_kernel(*refs, nb, act, has_bias, n_x2):
    refs = list(refs)
    x1_ref, w1_ref = refs[0], refs[1]
    pos = 2
    b1_ref = None
    if has_bias:
        b1_ref = refs[pos]
        pos += 1
    x2_refs = refs[pos:pos + n_x2] if n_x2 else None
    pos += n_x2
    w2_ref, o_ref = refs[pos], refs[pos + 1]
    _gated_mm_body(x1_ref, w1_ref, b1_ref, x2_refs, w2_ref, o_ref, nb=nb, act=act)


def _gated_mm(x1, w1, w2, *, x2s=None, b1=None, act, out_dtype, tm, tn):
    m, k1 = x1.shape
    nb, _, n = w1.shape
    k2 = w2.shape[1]
    in_specs = [pl.BlockSpec((tm, k1), lambda i, j: (i, 0)),
                pl.BlockSpec((nb, k1, tn), lambda i, j: (0, 0, j))]
    args = [x1, w1]
    if b1 is not None:
        in_specs.append(pl.BlockSpec((nb, 1, tn), lambda i, j: (0, 0, j)))
        args.append(b1)
    n_x2 = 0
    if x2s is not None:
        n_x2 = len(x2s)
        for x2 in x2s:
            in_specs.append(pl.BlockSpec((tm, k2), lambda i, j: (i, 0)))
            args.append(x2)
    in_specs.append(pl.BlockSpec((nb, k2, tn), lambda i, j: (0, 0, j)))
    args.append(w2)
    return pl.pallas_call(
        functools.partial(_gated_mm_kernel, nb=nb, act=act, has_bias=b1 is not None, n_x2=n_x2),
        grid=(m // tm, n // tn),
        in_specs=in_specs,
        out_specs=pl.BlockSpec((tm, tn), lambda i, j: (i, j)),
        out_shape=jax.ShapeDtypeStruct((m, n), out_dtype),
        compiler_params=_params("parallel", "arbitrary"),
        name="gated_matmul",
    )(*args)


def _grouped_gated_kernel(te_ref, nt_ref, x_ref, w1_ref, w2_ref, o_ref, w1_bf, w2_bf, *, act):
    i = pl.program_id(1)
    new_expert = jnp.logical_or(i == 0, te_ref[i] != te_ref[jnp.maximum(i - 1, 0)])

    @pl.when(new_expert)
    def _():
        w1_bf[...] = w1_ref[0].astype(BF16)
        w2_bf[...] = w2_ref[0].astype(BF16)

    @pl.when(i < nt_ref[0])
    def _():
        a = x_ref[...]
        z = jnp.dot(a, w1_bf[...], preferred_element_type=F32)
        lin = jnp.dot(a, w2_bf[...], preferred_element_type=F32)
        o_ref[...] = (_act(z, act) * lin).astype(o_ref.dtype)

    @pl.when(i >= nt_ref[0])
    def _():
        o_ref[...] = jnp.zeros_like(o_ref)


def _grouped_gated_mm(x, w1, w2, tile_expert, n_tiles, *, act, out_dtype, tm, tn):
    m, k = x.shape
    n = w1.shape[2]
    grid_spec = pltpu.PrefetchScalarGridSpec(
        num_scalar_prefetch=2,
        grid=(n // tn, m // tm),
        in_specs=[pl.BlockSpec((tm, k), lambda j, i, te, nt: (i, 0)),
                  pl.BlockSpec((1, k, tn), lambda j, i, te, nt: (te[i], 0, j)),
                  pl.BlockSpec((1, k, tn), lambda j, i, te, nt: (te[i], 0, j))],
        out_specs=pl.BlockSpec((tm, tn), lambda j, i, te, nt: (i, j)),
        scratch_shapes=[pltpu.VMEM((k, tn), BF16), pltpu.VMEM((k, tn), BF16)],
    )
    vmem_bytes = (2 * 2 * k * tn * 4 + 2 * k * tn * 2 + 2 * tm * k * 2
                  + 2 * tm * tn * jnp.dtype(out_dtype).itemsize + 2 * tm * tn * 4)
    return pl.pallas_call(
        functools.partial(_grouped_gated_kernel, act=act),
        grid_spec=grid_spec,
        out_shape=jax.ShapeDtypeStruct((m, n), out_dtype),
        compiler_params=_params("arbitrary", "arbitrary",
                                vmem_limit_bytes=max(V7X_VMEM_LIMIT_BYTES, vmem_bytes * 9 // 8)),
        name="grouped_gated_matmul",
    )(tile_expert, n_tiles, x, w1, w2)


def _grouped_mm_kernel(te_ref, nt_ref, x_ref, w_ref, o_ref):
    i = pl.program_id(0)

    @pl.when(i < nt_ref[0])
    def _():
        o_ref[...] = jnp.dot(x_ref[...], w_ref[0], preferred_element_type=F32)

    @pl.when(i >= nt_ref[0])
    def _():
        o_ref[...] = jnp.zeros_like(o_ref)


def _grouped_matmul(x, w, tile_expert, n_tiles, *, tm, tn):
    m, k = x.shape
    n = w.shape[2]
    grid_spec = pltpu.PrefetchScalarGridSpec(
        num_scalar_prefetch=2,
        grid=(m // tm, n // tn),
        in_specs=[pl.BlockSpec((tm, k), lambda i, j, te, nt: (i, 0)),
                  pl.BlockSpec((1, k, tn), lambda i, j, te, nt: (te[i], 0, j))],
        out_specs=pl.BlockSpec((tm, tn), lambda i, j, te, nt: (i, j)),
    )
    return pl.pallas_call(
        _grouped_mm_kernel,
        grid_spec=grid_spec,
        out_shape=jax.ShapeDtypeStruct((m, n), F32),
        compiler_params=_params("arbitrary", "arbitrary",
                                vmem_limit_bytes=_mm_vmem_limit(tm, k, tn, x.dtype, w.dtype, F32)),
        name="grouped_matmul",
    )(tile_expert, n_tiles, x, w)


def _proj_res_ln_kernel(h_ref, m_ref, w_ref, g_ref, b_ref, o_ref, obf_ref, *, alpha):
    y = alpha * h_ref[...] + jnp.dot(m_ref[...], w_ref[...], preferred_element_type=F32)
    out = _layer_norm(y, g_ref[...], b_ref[...], LN_EPS)
    o_ref[...] = out
    obf_ref[...] = out.astype(BF16)


def _proj_res_ln(h, m_in, w, g, b, *, alpha, tm):
    n, d = h.shape
    k = m_in.shape[1]
    return pl.pallas_call(
        functools.partial(_proj_res_ln_kernel, alpha=alpha),
        grid=(n // tm,),
        in_specs=[pl.BlockSpec((tm, d), lambda i: (i, 0)),
                  pl.BlockSpec((tm, k), lambda i: (i, 0)),
                  pl.BlockSpec((k, d), lambda i: (0, 0)),
                  pl.BlockSpec((1, d), lambda i: (0, 0)),
                  pl.BlockSpec((1, d), lambda i: (0, 0))],
        out_specs=[pl.BlockSpec((tm, d), lambda i: (i, 0)),
                   pl.BlockSpec((tm, d), lambda i: (i, 0))],
        out_shape=[jax.ShapeDtypeStruct((n, d), F32), jax.ShapeDtypeStruct((n, d), BF16)],
        compiler_params=_params("parallel"),
        name="proj_res_ln",
    )(h, m_in, w, g, b)


def _add_ln_kernel(*refs, alpha, n_terms, weighted):
    h_ref, hbf_ref, p_ref = refs[0:3]
    a_refs = refs[3:3 + n_terms]
    pos = 3 + n_terms
    c_ref = None
    if weighted:
        c_ref = refs[pos]
        pos += 1
    wg_ref, bg_ref, wp_ref, g_ref, b_ref, o_ref, obf_ref = refs[pos:pos + 7]
    gate = _sigmoid(jnp.dot(hbf_ref[...], wg_ref[...], preferred_element_type=F32) + bg_ref[...])
    ple = gate * jnp.dot(p_ref[...], wp_ref[...], preferred_element_type=F32)
    y = alpha * h_ref[...] + ple
    for j in range(n_terms):
        t = a_refs[j][...]
        if weighted:
            t = t * c_ref[:, j:j + 1]
        y = y + t
    out = _layer_norm(y, g_ref[...], b_ref[...], LN_EPS)
    o_ref[...] = out
    obf_ref[...] = out.astype(BF16)


def _add_ln(h, h_bf, p_bf, terms, coef, ple_wg, ple_bg, ple_wp, g, b, *, alpha, tm):
    n, d = h.shape
    dp = p_bf.shape[1]
    row = pl.BlockSpec((tm, d), lambda i: (i, 0))
    vec = pl.BlockSpec((1, d), lambda i: (0, 0))
    in_specs = [row, row, pl.BlockSpec((tm, dp), lambda i: (i, 0))] + [row] * len(terms)
    args = [h, h_bf, p_bf] + list(terms)
    if coef is not None:
        in_specs.append(pl.BlockSpec((tm, coef.shape[1]), lambda i: (i, 0)))
        args.append(coef)
    in_specs += [pl.BlockSpec((d, d), lambda i: (0, 0)), vec, pl.BlockSpec((dp, d), lambda i: (0, 0)), vec, vec]
    args += [ple_wg, ple_bg, ple_wp, g, b]
    return pl.pallas_call(
        functools.partial(_add_ln_kernel, alpha=alpha, n_terms=len(terms), weighted=coef is not None),
        grid=(n // tm,),
        in_specs=in_specs,
        out_specs=[row, row],
        out_shape=[jax.ShapeDtypeStruct((n, d), F32), jax.ShapeDtypeStruct((n, d), BF16)],
        compiler_params=_params("parallel"),
        name="add_ln",
    )(*args)


def _conv_kernel(u_ref, dw_ref, dwb_ref, g_ref, b_ref, o_ref, buf, shifted, *, ts, rb, width):
    s = pl.program_id(1)

    @pl.when(s == 0)
    def _():
        buf[0:CONV_HALO, :] = jnp.zeros((CONV_HALO, buf.shape[1]), F32)

    buf[CONV_HALO:CONV_HALO + ts, :] = u_ref[0]
    lead = CONV_HALO - (width - 1)
    span = shifted.shape[1]
    for r in range(1, SUBLANES):
        shifted[r - 1] = buf[r:r + span, :]
    for r0 in range(0, ts, rb):
        acc = jnp.broadcast_to(dwb_ref[...], (rb, buf.shape[1]))
        for j in range(width):
            m, r = divmod(lead + j, SUBLANES)
            lo = r0 + SUBLANES * m
            rows = buf[lo:lo + rb, :] if r == 0 else shifted[r - 1, lo:lo + rb, :]
            acc = acc + rows * dw_ref[SUBLANES * j:SUBLANES * (j + 1), :]
        y = _silu(_layer_norm(acc, g_ref[...], b_ref[...], LN_EPS))
        o_ref[0, r0:r0 + rb, :] = y.astype(o_ref.dtype)
    buf[0:CONV_HALO, :] = buf[ts:ts + CONV_HALO, :]


def _conv_module(u, dw, dw_b, ln_g, ln_b, *, ts, rb):
    bsz, seq, c = u.shape
    width = dw.shape[0]
    assert rb == SUBLANES
    dw_rep = jnp.repeat(dw.astype(F32), SUBLANES, axis=0)
    vec = pl.BlockSpec((1, c), lambda b, s: (0, 0))
    return pl.pallas_call(
        functools.partial(_conv_kernel, ts=ts, rb=rb, width=width),
        grid=(bsz, seq // ts),
        in_specs=[pl.BlockSpec((1, ts, c), lambda b, s: (b, s, 0)),
                  pl.BlockSpec((width * SUBLANES, c), lambda b, s: (0, 0)),
                  vec, vec, vec],
        out_specs=pl.BlockSpec((1, ts, c), lambda b, s: (b, s, 0)),
        out_shape=jax.ShapeDtypeStruct((bsz, seq, c), BF16),
        scratch_shapes=[pltpu.VMEM((CONV_HALO + ts, c), F32),
                        pltpu.VMEM((SUBLANES - 1, CONV_HALO + ts - SUBLANES, c), F32)],
        compiler_params=_params("parallel", "arbitrary"),
        name="conv_module",
    )(u, dw_rep, dw_b.reshape(1, c), ln_g.reshape(1, c), ln_b.reshape(1, c))


def _hgrn_steps(zq, zf, zi, zg, lb, ng, state_t, k_buf, b_buf, *, ts):
    nc = ts // CHUNK
    nsub = CHUNK // SUB
    q = _silu(zq) * (HGRN_EXPAND ** -0.5)
    k = (1.0 - lb) * _sigmoid(-zf)
    v = zi
    b = _mask_mm(_chunk_tril_mask(ts), jnp.log(1.0 - k))
    b2 = b * LOG2_E
    k_buf[...] = k
    b_buf[...] = b2

    row = lax.broadcasted_iota(jnp.int32, (CHUNK, CHUNK), 0)
    col = lax.broadcasted_iota(jnp.int32, (CHUNK, CHUNK), 1)
    row_sub = lax.shift_right_logical(row, SUB.bit_length() - 1)
    col_sub = lax.shift_right_logical(col, SUB.bit_length() - 1)
    earlier_sub = col_sub < row_sub
    same_sub_causal = jnp.logical_and(col_sub == row_sub, col <= row)
    strip_col = lax.broadcasted_iota(jnp.int32, (SUB, CHUNK), 1)
    yield

    o_intra, upd, decay, q_dec = [], [], [], []
    for c in range(nc):
        base = c * CHUNK
        q_c, k_c, b2_c, v_c = (t[base:base + CHUNK] for t in (q, k, b2, v))
        off_strips, diag_strips = [], []
        for i in range(nsub):
            lo = base + i * SUB
            q_i, b2_i = q[lo:lo + SUB], b2[lo:lo + SUB]
            if i == 0:
                off_strips.append(jnp.zeros((SUB, CHUNK), F32))
            else:
                beta = b_buf[lo - 1:lo, :]
                q_hat = q_i * jnp.exp2(b2_i - beta)
                k_hat = k_c * jnp.exp2(jnp.minimum(beta - b2_c, 0.0))
                off_strips.append(_bdot(q_hat, k_hat, "nt"))
            strip = jnp.zeros((SUB, CHUNK), F32)
            for j in range(SUB):
                sj = i * SUB + j
                k_row = k_buf[lo + j:lo + j + 1, :]
                b_row = b_buf[lo + j:lo + j + 1, :]
                e = jnp.exp2(b2_i - b_row)
                colv = jnp.sum(q_i * (k_row * e), axis=-1, keepdims=True)
                strip = jnp.where(strip_col == sj, colv, strip)
            diag_strips.append(strip)
            yield
        att =(jnp.where(earlier_sub, jnp.concatenate(off_strips, axis=0), 0.0)
               + jnp.where(same_sub_causal, jnp.concatenate(diag_strips, axis=0), 0.0))
        o_intra.append(_bdot(att, v_c, "nn"))
        b_last = b_buf[base + CHUNK - 1:base + CHUNK, :]
        upd.append(_bdot(v_c, k_c * jnp.exp2(b_last - b2_c), "tn"))
        decay.append(jnp.exp2(b_last))
        q_dec.append(q_c * jnp.exp2(b2_c))
        yield

    st = state_t[...]
    outs = []
    for c in range(nc):
        outs.append(o_intra[c] + _bdot(q_dec[c], st, "nt"))
        st = st * decay[c] + upd[c]
    state_t[...] = st
    yield
    o = jnp.concatenate(outs, axis=0)
    o = o * lax.rsqrt(jnp.mean(o * o, axis=-1, keepdims=True) + LN_EPS) * ng
    return o * _silu(zg)


def _token_shift(z, prev_row, mu):
    rows = lax.broadcasted_iota(jnp.int32, z.shape, 0)
    shifted = jnp.where(rows == 0, prev_row, pltpu.roll(z, 1, axis=0))
    return z + (shifted - z) * mu


def _rwkv_prep_kernel(*refs, vres, c):
    (zrkv_ref, zl_ref, mu_rkv_ref, mu_l_ref, w0_ref, w2_ref, a0_ref, a2_ref, g2_ref,
     kk_ref, ka_ref) = refs[:11]
    pos = 11
    if vres:
        v0_ref, v2_ref, vfirst_ref = refs[pos:pos + 3]
        pos += 3
    r_o, k_o, v_o, ld_o, kk_o, a_o, g_o, prev_rkv, prev_l = refs[pos:pos + 9]
    s = pl.program_id(1)

    @pl.when(s == 0)
    def _():
        prev_rkv[...] = jnp.zeros_like(prev_rkv)
        prev_l[...] = jnp.zeros_like(prev_l)

    z = zrkv_ref[0]
    zl = zl_ref[0]
    ts = z.shape[0]
    z_mix = _token_shift(z, prev_rkv[...], mu_rkv_ref[...])
    zl_mix = _token_shift(zl, prev_l[...], mu_l_ref[...])
    prev_rkv[...] = z[ts - 1:ts, :]
    prev_l[...] = zl[ts - 1:ts, :]

    zr, zk, zv = z_mix[:, 0:c], z_mix[:, c:2 * c], z_mix[:, 2 * c:3 * c]
    zw, za, zg = zl_mix[:, 0:128], zl_mix[:, 128:256], zl_mix[:, 256:512]

    def lora(act_in, w_ref):
        return jnp.dot(act_in.astype(BF16), w_ref[...], preferred_element_type=F32)

    x = -(w0_ref[...] + lora(jnp.tanh(zw), w2_ref))
    softplus = jnp.maximum(x, 0.0) + jnp.log(1.0 + jnp.exp(-jnp.abs(x)))
    w = -softplus - 0.5
    a = _sigmoid(a0_ref[...] + lora(za, a2_ref))
    g = lora(_sigmoid(zg), g2_ref)
    v = zv
    if vres:
        zm = zl_mix[:, 512:640]
        v = zv + (vfirst_ref[0].astype(F32) - zv) * _sigmoid(v0_ref[...] + lora(zm, v2_ref))
    r_o[0] = zr.astype(r_o.dtype)
    k_o[0] = (zk * (1.0 + (a - 1.0) * ka_ref[...])).astype(k_o.dtype)
    v_o[0] = v.astype(v_o.dtype)
    ld_o[0] = -jnp.exp(w)
    kk_o[0] = (zk * kk_ref[...]).astype(kk_o.dtype)
    a_o[0] = a.astype(a_o.dtype)
    g_o[0] = g.astype(g_o.dtype)


def _rwkv_prep(z_rkv, z_lora, mu_rkv, mu_l, w0, w2, a0, a2, g2, k_k, k_a, v0, v2, v_first, *, ts):
    bsz, seq, c3 = z_rkv.shape
    c = c3 // 3
    lw = z_lora.shape[2]
    vres = v0 is not None

    def full(arr):
        return pl.BlockSpec(arr.shape, lambda b, s: (0,) * arr.ndim)

    tile = pl.BlockSpec((1, ts, c), lambda b, s: (b, s, 0))
    args = [z_rkv, z_lora, mu_rkv, mu_l, w0, w2, a0, a2, g2, k_k, k_a]
    in_specs = [pl.BlockSpec((1, ts, c3), lambda b, s: (b, s, 0)),
                pl.BlockSpec((1, ts, lw), lambda b, s: (b, s, 0))] + [full(a) for a in args[2:]]
    if vres:
        args += [v0, v2, v_first]
        in_specs += [full(v0), full(v2), tile]
    return pl.pallas_call(
        functools.partial(_rwkv_prep_kernel, vres=vres, c=c),
        grid=(bsz, seq // ts),
        in_specs=in_specs,
        out_specs=[tile] * 7,
        out_shape=[jax.ShapeDtypeStruct((bsz, seq, c), dt) for dt in (BF16, BF16, BF16, F32, BF16, BF16, BF16)],
        scratch_shapes=[pltpu.VMEM((1, c3), F32), pltpu.VMEM((1, lw), F32)],
        compiler_params=_params("parallel", "arbitrary"),
        name="rwkv_prep",
    )(*args)


def _rwkv_steps(r, k, v, ld, kk, a, g, rk, lng, lnb, state, *, ts, hg):
    n = RWKV_HEAD
    nc = ts // CHUNK
    width = hg * n
    head_ones = _segment_ones_mask(width, n)
    ld = ld * LOG2_E
    lp = _mask_mm(_chunk_tril_mask(ts), ld)
    norm = jnp.sqrt(_mm_mask(kk * kk, head_ones))
    kkn = kk / jnp.maximum(norm, 1e-12)
    bvec = kkn * a
    inv_p = jnp.exp2(-lp)
    a_t = -kkn * jnp.exp2(lp - ld)
    r_t = r * jnp.exp2(lp)
    b_t = bvec * inv_p
    k_t = k * inv_p

    row = lax.broadcasted_iota(jnp.int32, (CHUNK, CHUNK), 0)
    col = lax.broadcasted_iota(jnp.int32, (CHUNK, CHUNK), 1)
    incl = row >= col
    strict = row > col
    eye = (row == col).astype(F32)

    cells = [(c, hh) for c in range(nc) for hh in range(hg)]

    def blk(x, c, hh):
        return x[c * CHUNK:(c + 1) * CHUNK, hh * n:(hh + 1) * n]

    p_last = [jnp.exp2(lp[(c + 1) * CHUNK - 1:(c + 1) * CHUNK, :]) for c in range(nc)]
    to_end = jnp.concatenate(
        [jnp.exp2(lp[(c + 1) * CHUNK - 1:(c + 1) * CHUNK, :] - lp[c * CHUNK:(c + 1) * CHUNK, :]) for c in range(nc)],
        axis=0)
    b_h = bvec * to_end
    k_h = k * to_end
    yield

    def stage(fn):
        out = {}
        for idx, cell in enumerate(cells):
            out[cell] = fn(cell)
            if idx % CELLS_PER_YIELD == CELLS_PER_YIELD - 1:
                yield
        return out

    a_c = {cell: blk(a_t, *cell) for cell in cells}
    v_c = {cell: blk(v, *cell) for cell in cells}
    gram = yield from stage(lambda cell: _bdot(
        jnp.concatenate([a_c[cell], blk(r_t, *cell)], axis=0),
        jnp.concatenate([blk(b_t, *cell), blk(k_t, *cell)], axis=0), "nt"))
    t_ab = {cell: jnp.where(strict, gram[cell][0:CHUNK, 0:CHUNK], 0.0) for cell in cells}
    t_ak = {cell: jnp.where(strict, gram[cell][0:CHUNK, CHUNK:], 0.0) for cell in cells}
    t_rb = {cell: jnp.where(incl, gram[cell][CHUNK:, 0:CHUNK], 0.0) for cell in cells}
    t_rk = {cell: jnp.where(incl, gram[cell][CHUNK:, CHUNK:], 0.0) for cell in cells}

    power = yield from stage(lambda cell: _bdot(t_ab[cell], t_ab[cell], "nn"))
    inv = {cell: eye + t_ab[cell] for cell in cells}
    for _ in range(CHUNK.bit_length() - 3):
        both = yield from stage(lambda cell: _bdot(
            jnp.concatenate([power[cell], inv[cell]], axis=0), power[cell], "nn"))
        power = {cell: both[cell][0:CHUNK] for cell in cells}
        inv = {cell: inv[cell] + both[cell][CHUNK:] for cell in cells}
    last = yield from stage(lambda cell: _bdot(inv[cell], power[cell], "nn"))
    inv = {cell: inv[cell] + last[cell] for cell in cells}

    tv = yield from stage(lambda cell: _bdot(
        jnp.concatenate([t_ak[cell], t_rk[cell]], axis=0), v_c[cell], "nn"))
    wu = yield from stage(lambda cell: _bdot(
        inv[cell], jnp.concatenate([a_c[cell], tv[cell][0:CHUNK]], axis=1), "nn"))
    qy = yield from stage(lambda cell: _bdot(t_rb[cell], wu[cell], "nn"))
    gh = yield from stage(lambda cell: _bdot(blk(b_h, *cell), wu[cell], "tn"))
    kv = yield from stage(lambda cell: _bdot(blk(k_h, *cell), v_c[cell], "tn"))
    q_g, y0, h_t = {}, {}, {}
    for cell in cells:
        c, hh = cell
        q_eff = blk(r_t, *cell) + qy[cell][:, 0:n]
        g_t = eye * p_last[c][:, hh * n:(hh + 1) * n] + gh[cell][:, 0:n]
        q_g[cell] = jnp.concatenate([q_eff, g_t], axis=0)
        y0[cell] = qy[cell][:, n:] + tv[cell][CHUNK:]
        h_t[cell] = gh[cell][:, n:] + kv[cell]
    yield

    z = [state[hh] for hh in range(hg)]
    y_rows = []
    for c in range(nc):
        y_heads = []
        for hh in range(hg):
            res = _bdot(q_g[(c, hh)], z[hh], "nn")
            y_heads.append(res[0:CHUNK] + y0[(c, hh)])
            z[hh] = res[CHUNK:] + h_t[(c, hh)]
        y_rows.append(jnp.concatenate(y_heads, axis=1))
        yield
    for hh in range(hg):
        state[hh] = z[hh]
    y = jnp.concatenate(y_rows, axis=0)

    inv_n = 1.0 / n
    mu = _mm_mask(y, head_ones) * inv_n
    yc = y - mu
    var = _mm_mask(yc * yc, head_ones) * inv_n
    yn = yc * lax.rsqrt(var + RWKV_LN_EPS) * lng + lnb
    bonus = _mm_mask(r * k * rk, head_ones) * v
    return (yn + bonus) * g


def _round_robin(gens):
    results = [None] * len(gens)
    live = list(range(len(gens)))
    while live:
        for idx in list(live):
            try:
                next(gens[idx])
            except StopIteration as stop:
                results[idx] = stop.value
                live.remove(idx)
    return results


def _dual_scan_kernel(r_ref, k_ref, v_ref, ld_ref, kk_ref, a_ref, g_ref, rk_ref, lng_ref, lnb_ref,
                      zq_ref, zf_ref, zi_ref, zg_ref, lb_ref, ng_ref,
                      o_rwkv_ref, o_hgrn_ref, r_state, h_state, k_buf, b_buf, *, ts, hg, hh):
    s = pl.program_id(2)

    @pl.when(s == 0)
    def _():
        r_state[...] = jnp.zeros_like(r_state)
        h_state[...] = jnp.zeros_like(h_state)

    dh = HGRN_EXPAND
    r, k, v, ld, kk, a, g = (ref[0].astype(F32) for ref in (r_ref, k_ref, v_ref, ld_ref, kk_ref, a_ref, g_ref))
    gens = [_rwkv_steps(r, k, v, ld, kk, a, g, rk_ref[...], lng_ref[...], lnb_ref[...], r_state, ts=ts, hg=hg)]
    for j in range(hh):
        cs = slice(j * dh, (j + 1) * dh)
        gens.append(_hgrn_steps(zq_ref[0, :, cs], zf_ref[0, :, cs], zi_ref[0, :, cs], zg_ref[0, :, cs],
                                lb_ref[:, cs], ng_ref[...], h_state.at[j], k_buf.at[j], b_buf.at[j], ts=ts))
    outs = _round_robin(gens)
    o_rwkv_ref[0] = outs[0].astype(o_rwkv_ref.dtype)
    o_hgrn_ref[0] = jnp.concatenate(outs[1:], axis=1).astype(o_hgrn_ref.dtype)


def _dual_scan(r, k, v, ld, kk, a, g, r_k, lnx_g, lnx_b, z_hgrn, lb, norm_g, *, ts, groups):
    bsz, seq, c = r.shape
    wr = c // groups
    hg = wr // RWKV_HEAD
    dh = HGRN_EXPAND
    ch = z_hgrn.shape[2] // 4
    wh = ch // groups
    hh = wh // dh
    rtile = pl.BlockSpec((1, ts, wr), lambda b, h, s: (b, s, h))
    rvec = pl.BlockSpec((1, wr), lambda b, h, s: (0, h))
    htile = pl.BlockSpec((1, ts, wh), lambda b, h, s: (b, s, h))

    def zspec(group):
        return pl.BlockSpec((1, ts, wh), lambda b, h, s: (b, s, group * groups + h))

    return pl.pallas_call(
        functools.partial(_dual_scan_kernel, ts=ts, hg=hg, hh=hh),
        grid=(bsz, groups, seq // ts),
        in_specs=[rtile] * 7 + [rvec] * 3 + [zspec(0), zspec(1), zspec(2), zspec(3),
                                             pl.BlockSpec((1, wh), lambda b, h, s: (0, h)),
                                             pl.BlockSpec((1, dh), lambda b, h, s: (0, 0))],
        out_specs=[rtile, htile],
        out_shape=[jax.ShapeDtypeStruct((bsz, seq, c), BF16), jax.ShapeDtypeStruct((bsz, seq, ch), BF16)],
        scratch_shapes=[pltpu.VMEM((hg, RWKV_HEAD, RWKV_HEAD), F32),
                        pltpu.VMEM((hh, dh, dh), F32),
                        pltpu.VMEM((hh, ts, dh), F32),
                        pltpu.VMEM((hh, ts, dh), F32)],
        compiler_params=_params("parallel", "parallel", "arbitrary"),
        name="dual_scan",
    )(r, k, v, ld, kk, a, g, r_k.reshape(1, c), lnx_g.reshape(1, c), lnx_b.reshape(1, c),
      z_hgrn, z_hgrn, z_hgrn, z_hgrn, lb.reshape(1, ch), norm_g.reshape(1, dh))


def _router_kernel(h_ref, w_ref, idx_ref, wt_ref, *, n_experts):
    logits = jnp.dot(h_ref[...], w_ref[...], precision=HI, preferred_element_type=F32)
    lane = lax.broadcasted_iota(jnp.int32, logits.shape, 1)
    neg = jnp.float32(-jnp.inf)
    big = jnp.int32(logits.shape[1])
    x = jnp.where(lane < n_experts, logits, neg)
    m1 = jnp.max(x, axis=-1, keepdims=True)
    i1 = jnp.min(jnp.where(x == m1, lane, big), axis=-1, keepdims=True)
    x2 = jnp.where(lane == i1, neg, x)
    m2 = jnp.max(x2, axis=-1, keepdims=True)
    i2 = jnp.min(jnp.where(x2 == m2, lane, big), axis=-1, keepdims=True)
    e2 = jnp.exp(m2 - m1)
    w1 = 1.0 / (1.0 + e2)
    w2 = e2 / (1.0 + e2)
    idx_ref[...] = jnp.where(lane == 0, i1, jnp.where(lane == 1, i2, 0))
    wt_ref[...] = jnp.where(lane == 0, w1, jnp.where(lane == 1, w2, 0.0))


def _router(h, w_router, *, tm):
    n, d = h.shape
    n_experts = w_router.shape[1]
    w_pad = jnp.zeros((d, 128), F32).at[:, :n_experts].set(w_router)
    out = pl.BlockSpec((tm, 128), lambda i: (i, 0))
    return pl.pallas_call(
        functools.partial(_router_kernel, n_experts=n_experts),
        grid=(n // tm,),
        in_specs=[pl.BlockSpec((tm, d), lambda i: (i, 0)),
                  pl.BlockSpec((d, 128), lambda i: (0, 0))],
        out_specs=[out, out],
        out_shape=[jax.ShapeDtypeStruct((n, 128), jnp.int32), jax.ShapeDtypeStruct((n, 128), F32)],
        compiler_params=_params("parallel"),
        name="moe_router",
    )(h, w_pad)


def _moe(h, h_bf, w_router, wg, wu, wd, *, tm):
    n, d = h.shape
    n_experts = wg.shape[0]
    idx_pad, wt_pad = _router(h, w_router, tm=512)
    e_flat = idx_pad[:, :TOP_K].reshape(-1)
    onehot = (e_flat[:, None] == jnp.arange(n_experts, dtype=jnp.int32)[None, :]).astype(jnp.int32)
    csum = jnp.cumsum(onehot, axis=0)
    rank = jnp.sum(csum * onehot, axis=1) - 1
    counts = csum[-1]
    padded = ((counts + tm - 1) // tm) * tm
    ends = jnp.cumsum(padded)
    starts = ends - padded
    pos = starts[e_flat] + rank
    n_rows = TOP_K * n + n_experts * tm
    src = jnp.zeros((n_rows,), jnp.int32).at[pos].set(jnp.arange(TOP_K * n, dtype=jnp.int32) // TOP_K)
    n_tiles = (ends[-1] // tm).astype(jnp.int32).reshape(1)
    tile_start = jnp.arange(n_rows // tm, dtype=jnp.int32) * tm
    tile_expert = jnp.minimum(jnp.searchsorted(ends, tile_start, side="right"), n_experts - 1).astype(jnp.int32)
    x_sorted = h_bf.at[src].get(mode="promise_in_bounds")
    h1 = _grouped_gated_mm(x_sorted, wg, wu, tile_expert, n_tiles, act="silu", out_dtype=BF16, tm=tm, tn=1024)
    y_sorted = _grouped_matmul(h1, wd, tile_expert, n_tiles, tm=tm, tn=1024)
    pos2 = pos.reshape(n, TOP_K)
    y0 = y_sorted.at[pos2[:, 0]].get(mode="promise_in_bounds")
    y1 = y_sorted.at[pos2[:, 1]].get(mode="promise_in_bounds")
    return [y0, y1], wt_pad


def _pad_cols(w, width):
    return jnp.pad(w, ((0, 0), (0, width - w.shape[1])))


def _pad_rows(w, height):
    return jnp.pad(w, ((0, height - w.shape[0]), (0, 0)))


def kernel(x, p, w_in, w_in_vres, conv_dw, conv_dw_b, conv_ln_g, conv_ln_b, rwkv_mu, rwkv_mu_vres, rwkv_w0, rwkv_w2, rwkv_a0, rwkv_a2, rwkv_v0, rwkv_v2, rwkv_g2, rwkv_kk, rwkv_ka, rwkv_rk, rwkv_lnx_g, rwkv_lnx_b, hgrn_lb, hgrn_norm_g, w_branch, b_gate, w_out, ln1_g, ln1_b, ffn_wg, ffn_wu, ffn_wd, moe_router, moe_wg, moe_wu, moe_wd, ple_wp, ple_wg, ple_bg, ln2_g, ln2_b):
    bsz, seq, d = x.shape
    depth = w_in.shape[0]
    n = bsz * seq
    cc = conv_dw.shape[2]
    cr = rwkv_w0.shape[1]
    ch = hgrn_lb.shape[1]
    d_decay, d_aaa, d_gate, d_mv = rwkv_w2.shape[1], rwkv_a2.shape[1], rwkv_g2.shape[1], rwkv_v2.shape[1]
    n_branch = w_branch.shape[1]
    alpha = float((2 * depth) ** 0.25)
    tm = min(1024, n)
    ts = min(256, seq)

    lb_soft = jax.nn.softmax(hgrn_lb.astype(F32), axis=0)
    lower_bounds = jnp.cumsum(lb_soft, axis=0) - lb_soft[0]

    h = x.reshape(n, d)
    h_bf = h.astype(BF16)
    v_first = None
    for i in range(depth):
        vres = i > 0
        wi = w_in[i]
        o_rwkv = 2 * cc
        o_lora = o_rwkv + 3 * cr
        o_hgrn = o_lora + d_decay + d_aaa + d_gate
        o_gate = o_hgrn + 4 * ch

        w_val = wi[:, 0:cc].astype(BF16)[None]
        w_glu = wi[:, cc:2 * cc].astype(BF16)[None]
        u = _gated_mm(h_bf, w_glu, w_val, act="sigmoid", out_dtype=F32, tm=tm, tn=min(1024, cc))
        y_conv = _conv_module(u.reshape(bsz, seq, cc), conv_dw[i], conv_dw_b[i], conv_ln_g[i], conv_ln_b[i],
                              ts=ts, rb=SUBLANES).reshape(n, cc)

        z_rkv = _matmul(h_bf, wi[:, o_rwkv:o_lora].astype(BF16), out_dtype=F32, tm=tm, tn=min(1024, cr))
        lora_cols = [_pad_cols(wi[:, o_lora:o_lora + d_decay], 128),
                     _pad_cols(wi[:, o_lora + d_decay:o_lora + d_decay + d_aaa], 128),
                     _pad_cols(wi[:, o_lora + d_decay + d_aaa:o_hgrn], 256)]
        mu_i = rwkv_mu[i]
        mu_cols = [_pad_cols(mu_i[None, 3 * cr:3 * cr + d_decay], 128),
                   _pad_cols(mu_i[None, 3 * cr + d_decay:3 * cr + d_decay + d_aaa], 128),
                   _pad_cols(mu_i[None, 3 * cr + d_decay + d_aaa:], 256)]
        if vres:
            lora_cols.append(_pad_cols(w_in_vres[i - 1], 128))
            mu_cols.append(_pad_cols(rwkv_mu_vres[i - 1][None], 128))
        w_lora = jnp.concatenate(lora_cols, axis=1).astype(BF16)
        mu_l = jnp.concatenate(mu_cols, axis=1)
        z_lora = _matmul(h_bf, w_lora, out_dtype=F32, tm=tm, tn=w_lora.shape[1])
        r, k, v, ld, kk, a, g = _rwkv_prep(
            z_rkv.reshape(bsz, seq, 3 * cr), z_lora.reshape(bsz, seq, -1),
            mu_i[None, :3 * cr], mu_l,
            rwkv_w0[i][None], _pad_rows(rwkv_w2[i], 128).astype(BF16),
            rwkv_a0[i][None], _pad_rows(rwkv_a2[i], 128).astype(BF16),
            _pad_rows(rwkv_g2[i], 256).astype(BF16),
            rwkv_kk[i][None], rwkv_ka[i][None],
            rwkv_v0[i - 1][None] if vres else None,
            _pad_rows(rwkv_v2[i - 1], 128).astype(BF16) if vres else None,
            v_first, ts=ts)
        if not vres:
            v_first = v
        z_hgrn = _matmul(h_bf, wi[:, o_hgrn:o_gate].astype(BF16), out_dtype=F32, tm=tm, tn=min(1024, 4 * ch))
        y_rwkv, y_hgrn = _dual_scan(r, k, v, ld, kk, a, g, rwkv_rk[i], rwkv_lnx_g[i], rwkv_lnx_b[i],
                                    z_hgrn.reshape(bsz, seq, 4 * ch), lower_bounds[i], hgrn_norm_g[i],
                                    ts=ts, groups=SCAN_GROUPS)
        y_rwkv = y_rwkv.reshape(n, cr)
        y_hgrn = y_hgrn.reshape(n, ch)

        w_g = wi[:, o_gate:o_gate + n_branch * d].reshape(d, n_branch, d).transpose(1, 0, 2).astype(BF16)
        merged = _gated_mm(h_bf, w_g, w_branch[i].astype(BF16), x2s=[y_conv, y_rwkv, y_hgrn],
                           b1=b_gate[i][:, None, :], act="sigmoid", out_dtype=BF16, tm=min(512, n), tn=min(512, d))
        h, h_bf = _proj_res_ln(h, merged, w_out[i].astype(BF16), ln1_g[i][None], ln1_b[i][None],
                               alpha=alpha, tm=min(256, n))

        j = i // 2
        if i % 2 == 0:
            h1 = _gated_mm(h_bf, ffn_wg[j].astype(BF16)[None], ffn_wu[j].astype(BF16)[None],
                           act="silu", out_dtype=BF16, tm=tm, tn=1024)
            terms, coef = [_matmul(h1, ffn_wd[j].astype(BF16), out_dtype=F32, tm=min(512, n), tn=1024)], None
        else:
            terms, coef = _moe(h, h_bf, moe_router[j], moe_wg[j], moe_wu[j],
                               moe_wd[j].astype(BF16), tm=min(512, n))
        h, h_bf = _add_ln(h, h_bf, p[i].reshape(n, -1).astype(BF16), terms, coef,
                          ple_wg[i].astype(BF16), ple_bg[i][None], ple_wp[i].astype(BF16),
                          ln2_g[i][None], ln2_b[i][None], alpha=alpha, tm=min(256, n))
    return h.reshape(bsz, seq, d)
```

```python
import functools

import jax
import jax.numpy as jnp
from jax import lax
from jax.experimental import pallas as pl
from jax.experimental.pallas import tpu as pltpu

F32 = jnp.float32
BF16 = jnp.bfloat16

V7X_VMEM_LIMIT_BYTES = 48 * 1024 * 1024

LN_EPS = 1e-5
RWKV_LN_EPS = 64e-5
RWKV_HEAD = 64
HGRN_EXPAND = 128
CHUNK = 64
SUB = 16
CONV_HALO = 32
SUBLANES = 8
LOG2_E = 1.4426950408889634
TOP_K = 2
SCAN_GROUPS = 4
CELLS_PER_YIELD = 4

NN_DIMS = (((1,), (0,)), ((), ()))
NT_DIMS = (((1,), (1,)), ((), ()))
TN_DIMS = (((0,), (0,)), ((), ()))


def _params(*semantics, vmem_limit_bytes=V7X_VMEM_LIMIT_BYTES):
    return pltpu.CompilerParams(dimension_semantics=semantics, vmem_limit_bytes=vmem_limit_bytes)


def _sigmoid(z):
    return 1.0 / (1.0 + jnp.exp(-z))


def _silu(z):
    return z * _sigmoid(z)


def _act(z, kind):
    return _sigmoid(z) if kind == "sigmoid" else _silu(z)


def _layer_norm(x, g, b, eps):
    mu = jnp.mean(x, axis=-1, keepdims=True)
    xc = x - mu
    var = jnp.mean(xc * xc, axis=-1, keepdims=True)
    return xc * lax.rsqrt(var + eps) * g + b


def _split_bf16(x, terms):
    parts = []
    rest = x
    for _ in range(terms):
        part = rest.astype(BF16).astype(F32)
        parts.append(part)
        rest = rest - part
    return parts


def _bdot(a, b, kind):
    dims = {"nn": NN_DIMS, "nt": NT_DIMS, "tn": TN_DIMS}[kind]
    return lax.dot_general(a.astype(BF16), b.astype(BF16), dims, preferred_element_type=F32)


def _mask_mm(mask, x):
    return sum(jnp.dot(mask, part.astype(BF16), preferred_element_type=F32) for part in _split_bf16(x, 2))


def _mm_mask(x, mask):
    return sum(jnp.dot(part.astype(BF16), mask, preferred_element_type=F32) for part in _split_bf16(x, 2))


def _chunk_tril_mask(ts):
    row = lax.broadcasted_iota(jnp.int32, (ts, ts), 0)
    col = lax.broadcasted_iota(jnp.int32, (ts, ts), 1)
    shift = CHUNK.bit_length() - 1
    same = lax.shift_right_logical(row, shift) == lax.shift_right_logical(col, shift)
    return jnp.logical_and(same, col <= row).astype(BF16)


def _segment_ones_mask(width, seg):
    row = lax.broadcasted_iota(jnp.int32, (width, width), 0)
    col = lax.broadcasted_iota(jnp.int32, (width, width), 1)
    shift = seg.bit_length() - 1
    return (lax.shift_right_logical(row, shift) == lax.shift_right_logical(col, shift)).astype(BF16)


def _mm_kernel(x_ref, w_ref, o_ref):
    o_ref[...] = jnp.dot(x_ref[...], w_ref[...], preferred_element_type=F32).astype(o_ref.dtype)


def _mm_vmem_limit(tm, k, tn, x_dtype, w_dtype, out_dtype):
    blocks = (tm * k * jnp.dtype(x_dtype).itemsize + k * tn * jnp.dtype(w_dtype).itemsize
              + tm * tn * jnp.dtype(out_dtype).itemsize)
    return max(V7X_VMEM_LIMIT_BYTES, (2 * blocks + tm * tn * 4) * 9 // 8)


def _matmul(x, w, *, out_dtype, tm, tn):
    m, k = x.shape
    n = w.shape[1]
    return pl.pallas_call(
        _mm_kernel,
        grid=(m // tm, n // tn),
        in_specs=[pl.BlockSpec((tm, k), lambda i, j: (i, 0)),
                  pl.BlockSpec((k, tn), lambda i, j: (0, j))],
        out_specs=pl.BlockSpec((tm, tn), lambda i, j: (i, j)),
        out_shape=jax.ShapeDtypeStruct((m, n), out_dtype),
        compiler_params=_params("parallel", "arbitrary",
                                vmem_limit_bytes=_mm_vmem_limit(tm, k, tn, x.dtype, w.dtype, out_dtype)),
        name="matmul",
    )(x, w)


def _gated_mm_body(x1_ref, w1_ref, b1_ref, x2_refs, w2_ref, o_ref, *, nb, act):
    a = x1_ref[...]
    acc = None
    for n in range(nb):
        z = jnp.dot(a, w1_ref[n], preferred_element_type=F32)
        if b1_ref is not None:
            z = z + b1_ref[n]
        lhs2 = a if x2_refs is None else x2_refs[n][...]
        lin = jnp.dot(lhs2, w2_ref[n], preferred_element_type=F32)
        t = _act(z, act) * lin
        acc = t if acc is None else acc + t
    o_ref[...] = acc.astype(o_ref.dtype)


def _gated_mm_kernel(*refs, nb, act, has_bias, n_x2):
    refs = list(refs)
    x1_ref, w1_ref = refs[0], refs[1]
    pos = 2
    b1_ref = None
    if has_bias:
        b1_ref = refs[pos]
        pos += 1
    x2_refs = refs[pos:pos + n_x2] if n_x2 else None
    pos += n_x2
    w2_ref, o_ref = refs[pos], refs[pos + 1]
    _gated_mm_body(x1_ref, w1_ref, b1_ref, x2_refs, w2_ref, o_ref, nb=nb, act=act)


def _gated_mm(x1, w1, w2, *, x2s=None, b1=None, act, out_dtype, tm, tn):
    m, k1 = x1.shape
    nb, _, n = w1.shape
    k2 = w2.shape[1]
    in_specs = [pl.BlockSpec((tm, k1), lambda i, j: (i, 0)),
                pl.BlockSpec((nb, k1, tn), lambda i, j: (0, 0, j))]
    args = [x1, w1]
    if b1 is not None:
        in_specs.append(pl.BlockSpec((nb, 1, tn), lambda i, j: (0, 0, j)))
        args.append(b1)
    n_x2 = 0
    if x2s is not None:
        n_x2 = len(x2s)
        for x2 in x2s:
            in_specs.append(pl.BlockSpec((tm, k2), lambda i, j: (i, 0)))
            args.append(x2)
    in_specs.append(pl.BlockSpec((nb, k2, tn), lambda i, j: (0, 0, j)))
    args.append(w2)
    return pl.pallas_call(
        functools.partial(_gated_mm_kernel, nb=nb, act=act, has_bias=b1 is not None, n_x2=n_x2),
        grid=(m // tm, n // tn),
        in_specs=in_specs,
        out_specs=pl.BlockSpec((tm, tn), lambda i, j: (i, j)),
        out_shape=jax.ShapeDtypeStruct((m, n), out_dtype),
        compiler_params=_params("parallel", "arbitrary"),
        name="gated_matmul",
    )(*args)


def _grouped_gated_kernel(te_ref, nt_ref, x_ref, w1_ref, w2_ref, o_ref, w1_bf, w2_bf, *, act):
    i = pl.program_id(1)
    new_expert = jnp.logical_or(i == 0, te_ref[i] != te_ref[jnp.maximum(i - 1, 0)])

    @pl.when(new_expert)
    def _():
        w1_bf[...] = w1_ref[0].astype(BF16)
        w2_bf[...] = w2_ref[0].astype(BF16)

    @pl.when(i < nt_ref[0])
    def _():
        a = x_ref[...]
        z = jnp.dot(a, w1_bf[...], preferred_element_type=F32)
        lin = jnp.dot(a, w2_bf[...], preferred_element_type=F32)
        o_ref[...] = (_act(z, act) * lin).astype(o_ref.dtype)

    @pl.when(i >= nt_ref[0])
    def _():
        o_ref[...] = jnp.zeros_like(o_ref)


def _grouped_gated_mm(x, w1, w2, tile_expert, n_tiles, *, act, out_dtype, tm, tn):
    m, k = x.shape
    n = w1.shape[2]
    grid_spec = pltpu.PrefetchScalarGridSpec(
        num_scalar_prefetch=2,
        grid=(n // tn, m // tm),
        in_specs=[pl.BlockSpec((tm, k), lambda j, i, te, nt: (i, 0)),
                  pl.BlockSpec((1, k, tn), lambda j, i, te, nt: (te[i], 0, j)),
                  pl.BlockSpec((1, k, tn), lambda j, i, te, nt: (te[i], 0, j))],
        out_specs=pl.BlockSpec((tm, tn), lambda j, i, te, nt: (i, j)),
        scratch_shapes=[pltpu.VMEM((k, tn), BF16), pltpu.VMEM((k, tn), BF16)],
    )
    vmem_bytes = (2 * 2 * k * tn * 4 + 2 * k * tn * 2 + 2 * tm * k * 2
                  + 2 * tm * tn * jnp.dtype(out_dtype).itemsize + 2 * tm * tn * 4)
    return pl.pallas_call(
        functools.partial(_grouped_gated_kernel, act=act),
        grid_spec=grid_spec,
        out_shape=jax.ShapeDtypeStruct((m, n), out_dtype),
        compiler_params=_params("arbitrary", "arbitrary",
                                vmem_limit_bytes=max(V7X_VMEM_LIMIT_BYTES, vmem_bytes * 9 // 8)),
        name="grouped_gated_matmul",
    )(tile_expert, n_tiles, x, w1, w2)


def _grouped_mm_kernel(te_ref, nt_ref, x_ref, w_ref, o_ref):
    i = pl.program_id(0)

    @pl.when(i < nt_ref[0])
    def _():
        o_ref[...] = jnp.dot(x_ref[...], w_ref[0], preferred_element_type=F32)

    @pl.when(i >= nt_ref[0])
    def _():
        o_ref[...] = jnp.zeros_like(o_ref)


def _grouped_matmul(x, w, tile_expert, n_tiles, *, tm, tn):
    m, k = x.shape
    n = w.shape[2]
    grid_spec = pltpu.PrefetchScalarGridSpec(
        num_scalar_prefetch=2,
        grid=(m // tm, n // tn),
        in_specs=[pl.BlockSpec((tm, k), lambda i, j, te, nt: (i, 0)),
                  pl.BlockSpec((1, k, tn), lambda i, j, te, nt: (te[i], 0, j))],
        out_specs=pl.BlockSpec((tm, tn), lambda i, j, te, nt: (i, j)),
    )
    return pl.pallas_call(
        _grouped_mm_kernel,
        grid_spec=grid_spec,
        out_shape=jax.ShapeDtypeStruct((m, n), F32),
        compiler_params=_params("arbitrary", "arbitrary",
                                vmem_limit_bytes=_mm_vmem_limit(tm, k, tn, x.dtype, w.dtype, F32)),
        name="grouped_matmul",
    )(tile_expert, n_tiles, x, w)


def _proj_res_ln_kernel(h_ref, m_ref, w_ref, g_ref, b_ref, o_ref, obf_ref, *, alpha):
    y = alpha * h_ref[...] + jnp.dot(m_ref[...], w_ref[...], preferred_element_type=F32)
    out = _layer_norm(y, g_ref[...], b_ref[...], LN_EPS)
    o_ref[...] = out
    obf_ref[...] = out.astype(BF16)


def _proj_res_ln(h, m_in, w, g, b, *, alpha, tm):
    n, d = h.shape
    k = m_in.shape[1]
    vmem_bytes = 2 * (tm * d * 4 + tm * k * 2 + tm * d * 6 + k * d * 2) + 2 * tm * d * 4
    return pl.pallas_call(
        functools.partial(_proj_res_ln_kernel, alpha=alpha),
        grid=(n // tm,),
        in_specs=[pl.BlockSpec((tm, d), lambda i: (i, 0)),
                  pl.BlockSpec((tm, k), lambda i: (i, 0)),
                  pl.BlockSpec((k, d), lambda i: (0, 0)),
                  pl.BlockSpec((1, d), lambda i: (0, 0)),
                  pl.BlockSpec((1, d), lambda i: (0, 0))],
        out_specs=[pl.BlockSpec((tm, d), lambda i: (i, 0)),
                   pl.BlockSpec((tm, d), lambda i: (i, 0))],
        out_shape=[jax.ShapeDtypeStruct((n, d), F32), jax.ShapeDtypeStruct((n, d), BF16)],
        compiler_params=_params("parallel", vmem_limit_bytes=max(V7X_VMEM_LIMIT_BYTES, vmem_bytes * 9 // 8)),
        name="proj_res_ln",
    )(h, m_in, w, g, b)


def _add_ln_kernel(*refs, alpha, n_terms, weighted):
    h_ref, hbf_ref, p_ref = refs[0:3]
    a_refs = refs[3:3 + n_terms]
    pos = 3 + n_terms
    c_ref = None
    if weighted:
        c_ref = refs[pos]
        pos += 1
    wg_ref, bg_ref, wp_ref, g_ref, b_ref, o_ref, obf_ref = refs[pos:pos + 7]
    gate = _sigmoid(jnp.dot(hbf_ref[...], wg_ref[...], preferred_element_type=F32) + bg_ref[...])
    ple = gate * jnp.dot(p_ref[...], wp_ref[...], preferred_element_type=F32)
    y = alpha * h_ref[...] + ple
    for j in range(n_terms):
        t = a_refs[j][...]
        if weighted:
            t = t * c_ref[:, j:j + 1]
        y = y + t
    out = _layer_norm(y, g_ref[...], b_ref[...], LN_EPS)
    o_ref[...] = out
    obf_ref[...] = out.astype(BF16)


def _add_ln(h, h_bf, p_bf, terms, coef, ple_wg, ple_bg, ple_wp, g, b, *, alpha, tm):
    n, d = h.shape
    dp = p_bf.shape[1]
    row = pl.BlockSpec((tm, d), lambda i: (i, 0))
    vec = pl.BlockSpec((1, d), lambda i: (0, 0))
    in_specs = [row, row, pl.BlockSpec((tm, dp), lambda i: (i, 0))] + [row] * len(terms)
    args = [h, h_bf, p_bf] + list(terms)
    if coef is not None:
        in_specs.append(pl.BlockSpec((tm, coef.shape[1]), lambda i: (i, 0)))
        args.append(coef)
    in_specs += [pl.BlockSpec((d, d), lambda i: (0, 0)), vec, pl.BlockSpec((dp, d), lambda i: (0, 0)), vec, vec]
    args += [ple_wg, ple_bg, ple_wp, g, b]
    return pl.pallas_call(
        functools.partial(_add_ln_kernel, alpha=alpha, n_terms=len(terms), weighted=coef is not None),
        grid=(n // tm,),
        in_specs=in_specs,
        out_specs=[row, row],
        out_shape=[jax.ShapeDtypeStruct((n, d), F32), jax.ShapeDtypeStruct((n, d), BF16)],
        compiler_params=_params("parallel"),
        name="add_ln",
    )(*args)


def _conv_kernel(u_ref, dw_ref, dwb_ref, g_ref, b_ref, o_ref, buf, shifted, *, ts, rb, width):
    s = pl.program_id(1)

    @pl.when(s == 0)
    def _():
        buf[0:CONV_HALO, :] = jnp.zeros((CONV_HALO, buf.shape[1]), F32)

    buf[CONV_HALO:CONV_HALO + ts, :] = u_ref[0]
    lead = CONV_HALO - (width - 1)
    span = shifted.shape[1]
    for r in range(1, SUBLANES):
        shifted[r - 1] = buf[r:r + span, :]
    for r0 in range(0, ts, rb):
        acc = jnp.broadcast_to(dwb_ref[...], (rb, buf.shape[1]))
        for j in range(width):
            m, r = divmod(lead + j, SUBLANES)
            lo = r0 + SUBLANES * m
            rows = buf[lo:lo + rb, :] if r == 0 else shifted[r - 1, lo:lo + rb, :]
            acc = acc + rows * dw_ref[SUBLANES * j:SUBLANES * (j + 1), :]
        y = _silu(_layer_norm(acc, g_ref[...], b_ref[...], LN_EPS))
        o_ref[0, r0:r0 + rb, :] = y.astype(o_ref.dtype)
    buf[0:CONV_HALO, :] = buf[ts:ts + CONV_HALO, :]


def _conv_module(u, dw, dw_b, ln_g, ln_b, *, ts, rb):
    bsz, seq, c = u.shape
    width = dw.shape[0]
    assert rb == SUBLANES
    dw_rep = jnp.repeat(dw.astype(F32), SUBLANES, axis=0)
    vec = pl.BlockSpec((1, c), lambda b, s: (0, 0))
    return pl.pallas_call(
        functools.partial(_conv_kernel, ts=ts, rb=rb, width=width),
        grid=(bsz, seq // ts),
        in_specs=[pl.BlockSpec((1, ts, c), lambda b, s: (b, s, 0)),
                  pl.BlockSpec((width * SUBLANES, c), lambda b, s: (0, 0)),
                  vec, vec, vec],
        out_specs=pl.BlockSpec((1, ts, c), lambda b, s: (b, s, 0)),
        out_shape=jax.ShapeDtypeStruct((bsz, seq, c), BF16),
        scratch_shapes=[pltpu.VMEM((CONV_HALO + ts, c), F32),
                        pltpu.VMEM((SUBLANES - 1, CONV_HALO + ts - SUBLANES, c), F32)],
        compiler_params=_params("parallel", "arbitrary"),
        name="conv_module",
    )(u, dw_rep, dw_b.reshape(1, c), ln_g.reshape(1, c), ln_b.reshape(1, c))


def _hgrn_steps(zq, zf, zi, zg, lb, ng, tril, state_t, k_buf, b_buf, *, ts):
    nc = ts // CHUNK
    nsub = CHUNK // SUB
    q = _silu(zq) * (HGRN_EXPAND ** -0.5)
    k = (1.0 - lb) * _sigmoid(-zf)
    v = zi
    b = _mask_mm(tril, jnp.log(1.0 - k))
    b2 = b * LOG2_E
    k_buf[...] = jnp.log2(k) - b2
    b_buf[...] = b2

    row = lax.broadcasted_iota(jnp.int32, (CHUNK, CHUNK), 0)
    col = lax.broadcasted_iota(jnp.int32, (CHUNK, CHUNK), 1)
    row_sub = lax.shift_right_logical(row, SUB.bit_length() - 1)
    col_sub = lax.shift_right_logical(col, SUB.bit_length() - 1)
    earlier_sub = col_sub < row_sub
    same_sub_causal = jnp.logical_and(col_sub == row_sub, col <= row)
    strip_col = lax.broadcasted_iota(jnp.int32, (SUB, CHUNK), 1)
    yield

    o_intra, upd, decay, q_dec = [], [], [], []
    for c in range(nc):
        base = c * CHUNK
        q_c, k_c, b2_c, v_c = (t[base:base + CHUNK] for t in (q, k, b2, v))
        off_strips, diag_strips = [], []
        for i in range(nsub):
            lo = base + i * SUB
            q_i, b2_i = q[lo:lo + SUB], b2[lo:lo + SUB]
            if i == 0:
                off_strips.append(jnp.zeros((SUB, CHUNK), F32))
            else:
                beta = b_buf[lo - 1:lo, :]
                q_hat = q_i * jnp.exp2(b2_i - beta)
                k_hat = k_c * jnp.exp2(jnp.minimum(beta - b2_c, 0.0))
                off_strips.append(_bdot(q_hat, k_hat, "nt"))
            strip = jnp.zeros((SUB, CHUNK), F32)
            for j in range(SUB):
                sj = i * SUB + j
                e = jnp.exp2(b2_i + k_buf[lo + j:lo + j + 1, :])
                colv = jnp.sum(q_i * e, axis=-1, keepdims=True)
                strip = jnp.where(strip_col == sj, colv, strip)
            diag_strips.append(strip)
            yield
        att =(jnp.where(earlier_sub, jnp.concatenate(off_strips, axis=0), 0.0)
               + jnp.where(same_sub_causal, jnp.concatenate(diag_strips, axis=0), 0.0))
        o_intra.append(_bdot(att, v_c, "nn"))
        b_last = b_buf[base + CHUNK - 1:base + CHUNK, :]
        upd.append(_bdot(v_c, k_c * jnp.exp2(b_last - b2_c), "tn"))
        decay.append(jnp.exp2(b_last))
        q_dec.append(q_c * jnp.exp2(b2_c))
        yield

    st = state_t[...]
    outs = []
    for c in range(nc):
        outs.append(o_intra[c] + _bdot(q_dec[c], st, "nt"))
        st = st * decay[c] + upd[c]
    state_t[...] = st
    yield
    o = jnp.concatenate(outs, axis=0)
    o = o * lax.rsqrt(jnp.mean(o * o, axis=-1, keepdims=True) + LN_EPS) * ng
    return o * _silu(zg)


def _token_shift(z, prev_row, mu):
    rows = lax.broadcasted_iota(jnp.int32, z.shape, 0)
    shifted = jnp.where(rows == 0, prev_row, pltpu.roll(z, 1, axis=0))
    return z + (shifted - z) * mu


def _rwkv_prep_kernel(*refs, vres, c):
    (zrkv_ref, zl_ref, mu_rkv_ref, mu_l_ref, w0_ref, w2_ref, a0_ref, a2_ref, g2_ref,
     kk_ref, ka_ref) = refs[:11]
    pos = 11
    if vres:
        v0_ref, v2_ref, vfirst_ref = refs[pos:pos + 3]
        pos += 3
    r_o, k_o, v_o, ld_o, kk_o, a_o, g_o, prev_rkv, prev_l = refs[pos:pos + 9]
    s = pl.program_id(1)

    @pl.when(s == 0)
    def _():
        prev_rkv[...] = jnp.zeros_like(prev_rkv)
        prev_l[...] = jnp.zeros_like(prev_l)

    z = zrkv_ref[0]
    zl = zl_ref[0]
    ts = z.shape[0]
    z_mix = _token_shift(z, prev_rkv[...], mu_rkv_ref[...])
    zl_mix = _token_shift(zl, prev_l[...], mu_l_ref[...])
    prev_rkv[...] = z[ts - 1:ts, :]
    prev_l[...] = zl[ts - 1:ts, :]

    zr, zk, zv = z_mix[:, 0:c], z_mix[:, c:2 * c], z_mix[:, 2 * c:3 * c]
    zw, za, zg = zl_mix[:, 0:128], zl_mix[:, 128:256], zl_mix[:, 256:512]

    def lora(act_in, w_ref):
        return jnp.dot(act_in.astype(BF16), w_ref[...], preferred_element_type=F32)

    x = -(w0_ref[...] + lora(jnp.tanh(zw), w2_ref))
    softplus = jnp.maximum(x, 0.0) + jnp.log(1.0 + jnp.exp(-jnp.abs(x)))
    w = -softplus - 0.5
    a = _sigmoid(a0_ref[...] + lora(za, a2_ref))
    g = lora(_sigmoid(zg), g2_ref)
    v = zv
    if vres:
        zm = zl_mix[:, 512:640]
        v = zv + (vfirst_ref[0].astype(F32) - zv) * _sigmoid(v0_ref[...] + lora(zm, v2_ref))
    r_o[0] = zr.astype(r_o.dtype)
    k_o[0] = (zk * (1.0 + (a - 1.0) * ka_ref[...])).astype(k_o.dtype)
    v_o[0] = v.astype(v_o.dtype)
    ld_o[0] = -jnp.exp(w)
    kk_o[0] = (zk * kk_ref[...]).astype(kk_o.dtype)
    a_o[0] = a.astype(a_o.dtype)
    g_o[0] = g.astype(g_o.dtype)


def _rwkv_prep(z_rkv, z_lora, mu_rkv, mu_l, w0, w2, a0, a2, g2, k_k, k_a, v0, v2, v_first, *, ts):
    bsz, seq, c3 = z_rkv.shape
    c = c3 // 3
    lw = z_lora.shape[2]
    vres = v0 is not None

    def full(arr):
        return pl.BlockSpec(arr.shape, lambda b, s: (0,) * arr.ndim)

    tile = pl.BlockSpec((1, ts, c), lambda b, s: (b, s, 0))
    args = [z_rkv, z_lora, mu_rkv, mu_l, w0, w2, a0, a2, g2, k_k, k_a]
    in_specs = [pl.BlockSpec((1, ts, c3), lambda b, s: (b, s, 0)),
                pl.BlockSpec((1, ts, lw), lambda b, s: (b, s, 0))] + [full(a) for a in args[2:]]
    if vres:
        args += [v0, v2, v_first]
        in_specs += [full(v0), full(v2), tile]
    return pl.pallas_call(
        functools.partial(_rwkv_prep_kernel, vres=vres, c=c),
        grid=(bsz, seq // ts),
        in_specs=in_specs,
        out_specs=[tile] * 7,
        out_shape=[jax.ShapeDtypeStruct((bsz, seq, c), dt) for dt in (BF16, BF16, BF16, F32, BF16, BF16, BF16)],
        scratch_shapes=[pltpu.VMEM((1, c3), F32), pltpu.VMEM((1, lw), F32)],
        compiler_params=_params("parallel", "arbitrary"),
        name="rwkv_prep",
    )(*args)


def _rwkv_steps(r, k, v, ld, kk, a, g, rk, lng, lnb, tril, head_ones, state, *, ts, hg):
    n = RWKV_HEAD
    nc = ts // CHUNK
    width = hg * n
    ld = ld * LOG2_E
    lp = _mask_mm(tril, ld)
    norm = jnp.sqrt(_mm_mask(kk * kk, head_ones))
    kkn = kk / jnp.maximum(norm, 1e-12)
    bvec = kkn * a
    inv_p = jnp.exp2(-lp)
    a_t = -kkn * jnp.exp2(lp - ld)
    r_t = r * jnp.exp2(lp)
    b_t = bvec * inv_p
    k_t = k * inv_p

    row = lax.broadcasted_iota(jnp.int32, (CHUNK, CHUNK), 0)
    col = lax.broadcasted_iota(jnp.int32, (CHUNK, CHUNK), 1)
    incl = row >= col
    strict = row > col
    eye = (row == col).astype(F32)

    cells = [(c, hh) for c in range(nc) for hh in range(hg)]

    def blk(x, c, hh):
        return x[c * CHUNK:(c + 1) * CHUNK, hh * n:(hh + 1) * n]

    p_last = [jnp.exp2(lp[(c + 1) * CHUNK - 1:(c + 1) * CHUNK, :]) for c in range(nc)]
    to_end = jnp.concatenate(
        [jnp.exp2(lp[(c + 1) * CHUNK - 1:(c + 1) * CHUNK, :] - lp[c * CHUNK:(c + 1) * CHUNK, :]) for c in range(nc)],
        axis=0)
    b_h = bvec * to_end
    k_h = k * to_end
    yield

    def stage(fn):
        out = {}
        for idx, cell in enumerate(cells):
            out[cell] = fn(cell)
            if idx % CELLS_PER_YIELD == CELLS_PER_YIELD - 1:
                yield
        return out

    a_c = {cell: blk(a_t, *cell) for cell in cells}
    v_c = {cell: blk(v, *cell) for cell in cells}
    gram = yield from stage(lambda cell: _bdot(
        jnp.concatenate([a_c[cell], blk(r_t, *cell)], axis=0),
        jnp.concatenate([blk(b_t, *cell), blk(k_t, *cell)], axis=0), "nt"))
    t_ab = {cell: jnp.where(strict, gram[cell][0:CHUNK, 0:CHUNK], 0.0) for cell in cells}
    t_ak = {cell: jnp.where(strict, gram[cell][0:CHUNK, CHUNK:], 0.0) for cell in cells}
    t_rb = {cell: jnp.where(incl, gram[cell][CHUNK:, 0:CHUNK], 0.0) for cell in cells}
    t_rk = {cell: jnp.where(incl, gram[cell][CHUNK:, CHUNK:], 0.0) for cell in cells}

    power = yield from stage(lambda cell: _bdot(t_ab[cell], t_ab[cell], "nn"))
    inv = {cell: eye + t_ab[cell] for cell in cells}
    for _ in range(CHUNK.bit_length() - 3):
        both = yield from stage(lambda cell: _bdot(
            jnp.concatenate([power[cell], inv[cell]], axis=0), power[cell], "nn"))
        power = {cell: both[cell][0:CHUNK] for cell in cells}
        inv = {cell: inv[cell] + both[cell][CHUNK:] for cell in cells}
    last = yield from stage(lambda cell: _bdot(inv[cell], power[cell], "nn"))
    inv = {cell: inv[cell] + last[cell] for cell in cells}

    tv = yield from stage(lambda cell: _bdot(
        jnp.concatenate([t_ak[cell], t_rk[cell]], axis=0), v_c[cell], "nn"))
    wu = yield from stage(lambda cell: _bdot(
        inv[cell], jnp.concatenate([a_c[cell], tv[cell][0:CHUNK]], axis=1), "nn"))
    qy = yield from stage(lambda cell: _bdot(t_rb[cell], wu[cell], "nn"))
    gh = yield from stage(lambda cell: _bdot(blk(b_h, *cell), wu[cell], "tn"))
    kv = yield from stage(lambda cell: _bdot(blk(k_h, *cell), v_c[cell], "tn"))
    q_g, y0, h_t = {}, {}, {}
    for cell in cells:
        c, hh = cell
        q_eff = blk(r_t, *cell) + qy[cell][:, 0:n]
        g_t = eye * p_last[c][:, hh * n:(hh + 1) * n] + gh[cell][:, 0:n]
        q_g[cell] = jnp.concatenate([q_eff, g_t], axis=0)
        y0[cell] = qy[cell][:, n:] + tv[cell][CHUNK:]
        h_t[cell] = gh[cell][:, n:] + kv[cell]
    yield

    z = [state[hh] for hh in range(hg)]
    y_rows = []
    for c in range(nc):
        y_heads = []
        for hh in range(hg):
            res = _bdot(q_g[(c, hh)], z[hh], "nn")
            y_heads.append(res[0:CHUNK] + y0[(c, hh)])
            z[hh] = res[CHUNK:] + h_t[(c, hh)]
        y_rows.append(jnp.concatenate(y_heads, axis=1))
        yield
    for hh in range(hg):
        state[hh] = z[hh]
    y = jnp.concatenate(y_rows, axis=0)

    inv_n = 1.0 / n
    mu = _mm_mask(y, head_ones) * inv_n
    yc = y - mu
    var = _mm_mask(yc * yc, head_ones) * inv_n
    yn = yc * lax.rsqrt(var + RWKV_LN_EPS) * lng + lnb
    bonus = _mm_mask(r * k * rk, head_ones) * v
    return (yn + bonus) * g


def _round_robin(gens):
    results = [None] * len(gens)
    live = list(range(len(gens)))
    while live:
        for idx in list(live):
            try:
                next(gens[idx])
            except StopIteration as stop:
                results[idx] = stop.value
                live.remove(idx)
    return results


def _dual_scan_kernel(r_ref, k_ref, v_ref, ld_ref, kk_ref, a_ref, g_ref, rk_ref, lng_ref, lnb_ref,
                      zq_ref, zf_ref, zi_ref, zg_ref, lb_ref, ng_ref, tril_ref, head_ones_ref,
                      o_rwkv_ref, o_hgrn_ref, r_state, h_state, k_buf, b_buf, *, ts, hg, hh):
    s = pl.program_id(2)

    @pl.when(s == 0)
    def _():
        r_state[...] = jnp.zeros_like(r_state)
        h_state[...] = jnp.zeros_like(h_state)

    dh = HGRN_EXPAND
    r, k, v, ld, kk, a, g = (ref[0].astype(F32) for ref in (r_ref, k_ref, v_ref, ld_ref, kk_ref, a_ref, g_ref))
    tril = tril_ref[...]
    gens = [_rwkv_steps(r, k, v, ld, kk, a, g, rk_ref[...], lng_ref[...], lnb_ref[...], tril, head_ones_ref[...],
                        r_state, ts=ts, hg=hg)]
    for j in range(hh):
        cs = slice(j * dh, (j + 1) * dh)
        gens.append(_hgrn_steps(zq_ref[0, :, cs], zf_ref[0, :, cs], zi_ref[0, :, cs], zg_ref[0, :, cs],
                                lb_ref[:, cs], ng_ref[...], tril, h_state.at[j], k_buf.at[j], b_buf.at[j], ts=ts))
    outs = _round_robin(gens)
    o_rwkv_ref[0] = outs[0].astype(o_rwkv_ref.dtype)
    o_hgrn_ref[0] = jnp.concatenate(outs[1:], axis=1).astype(o_hgrn_ref.dtype)


def _dual_scan(r, k, v, ld, kk, a, g, r_k, lnx_g, lnx_b, z_hgrn, lb, norm_g, *, ts, groups):
    bsz, seq, c = r.shape
    wr = c // groups
    hg = wr // RWKV_HEAD
    dh = HGRN_EXPAND
    ch = z_hgrn.shape[2] // 4
    wh = ch // groups
    hh = wh // dh
    rtile = pl.BlockSpec((1, ts, wr), lambda b, h, s: (b, s, h))
    rvec = pl.BlockSpec((1, wr), lambda b, h, s: (0, h))
    htile = pl.BlockSpec((1, ts, wh), lambda b, h, s: (b, s, h))

    def zspec(group):
        return pl.BlockSpec((1, ts, wh), lambda b, h, s: (b, s, group * groups + h))

    return pl.pallas_call(
        functools.partial(_dual_scan_kernel, ts=ts, hg=hg, hh=hh),
        grid=(bsz, groups, seq // ts),
        in_specs=[rtile] * 7 + [rvec] * 3 + [zspec(0), zspec(1), zspec(2), zspec(3),
                                             pl.BlockSpec((1, wh), lambda b, h, s: (0, h)),
                                             pl.BlockSpec((1, dh), lambda b, h, s: (0, 0)),
                                             pl.BlockSpec((ts, ts), lambda b, h, s: (0, 0)),
                                             pl.BlockSpec((wr, wr), lambda b, h, s: (0, 0))],
        out_specs=[rtile, htile],
        out_shape=[jax.ShapeDtypeStruct((bsz, seq, c), BF16), jax.ShapeDtypeStruct((bsz, seq, ch), BF16)],
        scratch_shapes=[pltpu.VMEM((hg, RWKV_HEAD, RWKV_HEAD), F32),
                        pltpu.VMEM((hh, dh, dh), F32),
                        pltpu.VMEM((hh, ts, dh), F32),
                        pltpu.VMEM((hh, ts, dh), F32)],
        compiler_params=_params("parallel", "parallel", "arbitrary"),
        name="dual_scan",
    )(r, k, v, ld, kk, a, g, r_k.reshape(1, c), lnx_g.reshape(1, c), lnx_b.reshape(1, c),
      z_hgrn, z_hgrn, z_hgrn, z_hgrn, lb.reshape(1, ch), norm_g.reshape(1, dh),
      _chunk_tril_mask(ts), _segment_ones_mask(wr, RWKV_HEAD))


def _router_kernel(h_ref, w_ref, idx_ref, wt_ref, *, n_experts):
    h_hi, h_lo = (t.astype(BF16) for t in _split_bf16(h_ref[...], 2))
    w_hi, w_lo = (t.astype(BF16) for t in _split_bf16(w_ref[...], 2))
    logits = (jnp.dot(h_hi, w_hi, preferred_element_type=F32) + jnp.dot(h_lo, w_hi, preferred_element_type=F32)
              + jnp.dot(h_hi, w_lo, preferred_element_type=F32))
    lane = lax.broadcasted_iota(jnp.int32, logits.shape, 1)
    neg = jnp.float32(-jnp.inf)
    big = jnp.int32(logits.shape[1])
    x = jnp.where(lane < n_experts, logits, neg)
    m1 = jnp.max(x, axis=-1, keepdims=True)
    i1 = jnp.min(jnp.where(x == m1, lane, big), axis=-1, keepdims=True)
    x2 = jnp.where(lane == i1, neg, x)
    m2 = jnp.max(x2, axis=-1, keepdims=True)
    i2 = jnp.min(jnp.where(x2 == m2, lane, big), axis=-1, keepdims=True)
    e2 = jnp.exp(m2 - m1)
    w1 = 1.0 / (1.0 + e2)
    w2 = e2 / (1.0 + e2)
    idx_ref[...] = jnp.where(lane == 0, i1, jnp.where(lane == 1, i2, 0))
    wt_ref[...] = jnp.where(lane == 0, w1, jnp.where(lane == 1, w2, 0.0))


def _router(h, w_router, *, tm):
    n, d = h.shape
    n_experts = w_router.shape[1]
    w_pad = jnp.zeros((d, 128), F32).at[:, :n_experts].set(w_router)
    out = pl.BlockSpec((tm, 128), lambda i: (i, 0))
    return pl.pallas_call(
        functools.partial(_router_kernel, n_experts=n_experts),
        grid=(n // tm,),
        in_specs=[pl.BlockSpec((tm, d), lambda i: (i, 0)),
                  pl.BlockSpec((d, 128), lambda i: (0, 0))],
        out_specs=[out, out],
        out_shape=[jax.ShapeDtypeStruct((n, 128), jnp.int32), jax.ShapeDtypeStruct((n, 128), F32)],
        compiler_params=_params("parallel"),
        name="moe_router",
    )(h, w_pad)


def _moe(h, h_bf, w_router, wg, wu, wd, *, tm):
    n, d = h.shape
    n_experts = wg.shape[0]
    idx_pad, wt_pad = _router(h, w_router, tm=512)
    e_flat = idx_pad[:, :TOP_K].reshape(-1)
    onehot = (e_flat[:, None] == jnp.arange(n_experts, dtype=jnp.int32)[None, :]).astype(jnp.int32)
    csum = jnp.cumsum(onehot, axis=0)
    rank = jnp.sum(csum * onehot, axis=1) - 1
    counts = csum[-1]
    padded = ((counts + tm - 1) // tm) * tm
    ends = jnp.cumsum(padded)
    starts = ends - padded
    pos = starts[e_flat] + rank
    n_rows = TOP_K * n + n_experts * tm
    src = jnp.zeros((n_rows,), jnp.int32).at[pos].set(jnp.arange(TOP_K * n, dtype=jnp.int32) // TOP_K)
    n_tiles = (ends[-1] // tm).astype(jnp.int32).reshape(1)
    tile_start = jnp.arange(n_rows // tm, dtype=jnp.int32) * tm
    tile_expert = jnp.minimum(jnp.searchsorted(ends, tile_start, side="right"), n_experts - 1).astype(jnp.int32)
    x_sorted = h_bf.at[src].get(mode="promise_in_bounds")
    h1 = _grouped_gated_mm(x_sorted, wg, wu, tile_expert, n_tiles, act="silu", out_dtype=BF16, tm=tm, tn=1024)
    y_sorted = _grouped_matmul(h1, wd, tile_expert, n_tiles, tm=tm, tn=1024)
    pos2 = pos.reshape(n, TOP_K)
    y0 = y_sorted.at[pos2[:, 0]].get(mode="promise_in_bounds")
    y1 = y_sorted.at[pos2[:, 1]].get(mode="promise_in_bounds")
    return [y0, y1], wt_pad


def _pad_cols(w, width):
    return jnp.pad(w, ((0, 0), (0, width - w.shape[1])))


def _pad_rows(w, height):
    return jnp.pad(w, ((0, height - w.shape[0]), (0, 0)))


def kernel(x, p, w_in, w_in_vres, conv_dw, conv_dw_b, conv_ln_g, conv_ln_b, rwkv_mu, rwkv_mu_vres, rwkv_w0, rwkv_w2, rwkv_a0, rwkv_a2, rwkv_v0, rwkv_v2, rwkv_g2, rwkv_kk, rwkv_ka, rwkv_rk, rwkv_lnx_g, rwkv_lnx_b, hgrn_lb, hgrn_norm_g, w_branch, b_gate, w_out, ln1_g, ln1_b, ffn_wg, ffn_wu, ffn_wd, moe_router, moe_wg, moe_wu, moe_wd, ple_wp, ple_wg, ple_bg, ln2_g, ln2_b):
    bsz, seq, d = x.shape
    depth = w_in.shape[0]
    n = bsz * seq
    cc = conv_dw.shape[2]
    cr = rwkv_w0.shape[1]
    ch = hgrn_lb.shape[1]
    d_decay, d_aaa, d_gate, d_mv = rwkv_w2.shape[1], rwkv_a2.shape[1], rwkv_g2.shape[1], rwkv_v2.shape[1]
    n_branch = w_branch.shape[1]
    alpha = float((2 * depth) ** 0.25)
    tm = min(1024, n)
    ts = min(256, seq)

    lb_soft = jax.nn.softmax(hgrn_lb.astype(F32), axis=0)
    lower_bounds = jnp.cumsum(lb_soft, axis=0) - lb_soft[0]

    h = x.reshape(n, d)
    h_bf = h.astype(BF16)
    v_first = None
    for i in range(depth):
        vres = i > 0
        wi = w_in[i]
        o_rwkv = 2 * cc
        o_lora = o_rwkv + 3 * cr
        o_hgrn = o_lora + d_decay + d_aaa + d_gate
        o_gate = o_hgrn + 4 * ch

        w_val = wi[:, 0:cc].astype(BF16)[None]
        w_glu = wi[:, cc:2 * cc].astype(BF16)[None]
        u = _gated_mm(h_bf, w_glu, w_val, act="sigmoid", out_dtype=F32, tm=tm, tn=min(1024, cc))
        y_conv = _conv_module(u.reshape(bsz, seq, cc), conv_dw[i], conv_dw_b[i], conv_ln_g[i], conv_ln_b[i],
                              ts=ts, rb=SUBLANES).reshape(n, cc)

        z_rkv = _matmul(h_bf, wi[:, o_rwkv:o_lora].astype(BF16), out_dtype=F32, tm=tm, tn=min(1024, cr))
        lora_cols = [_pad_cols(wi[:, o_lora:o_lora + d_decay], 128),
                     _pad_cols(wi[:, o_lora + d_decay:o_lora + d_decay + d_aaa], 128),
                     _pad_cols(wi[:, o_lora + d_decay + d_aaa:o_hgrn], 256)]
        mu_i = rwkv_mu[i]
        mu_cols = [_pad_cols(mu_i[None, 3 * cr:3 * cr + d_decay], 128),
                   _pad_cols(mu_i[None, 3 * cr + d_decay:3 * cr + d_decay + d_aaa], 128),
                   _pad_cols(mu_i[None, 3 * cr + d_decay + d_aaa:], 256)]
        if vres:
            lora_cols.append(_pad_cols(w_in_vres[i - 1], 128))
            mu_cols.append(_pad_cols(rwkv_mu_vres[i - 1][None], 128))
        w_lora = jnp.concatenate(lora_cols, axis=1).astype(BF16)
        mu_l = jnp.concatenate(mu_cols, axis=1)
        z_lora = _matmul(h_bf, w_lora, out_dtype=F32, tm=tm, tn=w_lora.shape[1])
        r, k, v, ld, kk, a, g = _rwkv_prep(
            z_rkv.reshape(bsz, seq, 3 * cr), z_lora.reshape(bsz, seq, -1),
            mu_i[None, :3 * cr], mu_l,
            rwkv_w0[i][None], _pad_rows(rwkv_w2[i], 128).astype(BF16),
            rwkv_a0[i][None], _pad_rows(rwkv_a2[i], 128).astype(BF16),
            _pad_rows(rwkv_g2[i], 256).astype(BF16),
            rwkv_kk[i][None], rwkv_ka[i][None],
            rwkv_v0[i - 1][None] if vres else None,
            _pad_rows(rwkv_v2[i - 1], 128).astype(BF16) if vres else None,
            v_first, ts=ts)
        if not vres:
            v_first = v
        z_hgrn = _matmul(h_bf, wi[:, o_hgrn:o_gate].astype(BF16), out_dtype=F32, tm=tm, tn=min(1024, 4 * ch))
        y_rwkv, y_hgrn = _dual_scan(r, k, v, ld, kk, a, g, rwkv_rk[i], rwkv_lnx_g[i], rwkv_lnx_b[i],
                                    z_hgrn.reshape(bsz, seq, 4 * ch), lower_bounds[i], hgrn_norm_g[i],
                                    ts=ts, groups=SCAN_GROUPS)
        y_rwkv = y_rwkv.reshape(n, cr)
        y_hgrn = y_hgrn.reshape(n, ch)

        w_g = wi[:, o_gate:o_gate + n_branch * d].reshape(d, n_branch, d).transpose(1, 0, 2).astype(BF16)
        merged = _gated_mm(h_bf, w_g, w_branch[i].astype(BF16), x2s=[y_conv, y_rwkv, y_hgrn],
                           b1=b_gate[i][:, None, :], act="sigmoid", out_dtype=BF16, tm=min(512, n), tn=min(512, d))
        h, h_bf = _proj_res_ln(h, merged, w_out[i].astype(BF16), ln1_g[i][None], ln1_b[i][None],
                               alpha=alpha, tm=min(512, n))

        j = i // 2
        if i % 2 == 0:
            h1 = _gated_mm(h_bf, ffn_wg[j].astype(BF16)[None], ffn_wu[j].astype(BF16)[None],
                           act="silu", out_dtype=BF16, tm=tm, tn=1024)
            terms, coef = [_matmul(h1, ffn_wd[j].astype(BF16), out_dtype=F32, tm=min(512, n), tn=1024)], None
        else:
            terms, coef = _moe(h, h_bf, moe_router[j], moe_wg[j], moe_wu[j],
                               moe_wd[j].astype(BF16), tm=min(512, n))
        h, h_bf = _add_ln(h, h_bf, p[i].reshape(n, -1).astype(BF16), terms, coef,
                          ple_wg[i].astype(BF16), ple_bg[i][None], ple_wp[i].astype(BF16),
                          ln2_g[i][None], ln2_b[i][None], alpha=alpha, tm=min(256, n))
    return h.reshape(bsz, seq, d)
```

```python
import functools

import jax
import jax.numpy as jnp
from jax import lax
from jax.experimental import pallas as pl
from jax.experimental.pallas import tpu as pltpu

F32 = jnp.float32
BF16 = jnp.bfloat16

V7X_VMEM_LIMIT_BYTES = 48 * 1024 * 1024

LN_EPS = 1e-5
RWKV_LN_EPS = 64e-5
RWKV_HEAD = 64
HGRN_EXPAND = 128
CHUNK = 64
SUB = 16
CONV_HALO = 32
SUBLANES = 8
LOG2_E = 1.4426950408889634
TOP_K = 2
SCAN_GROUPS = 4
CELLS_PER_YIELD = 4

NN_DIMS = (((1,), (0,)), ((), ()))
NT_DIMS = (((1,), (1,)), ((), ()))
TN_DIMS = (((0,), (0,)), ((), ()))


def _params(*semantics, vmem_limit_bytes=V7X_VMEM_LIMIT_BYTES):
    return pltpu.CompilerParams(dimension_semantics=semantics, vmem_limit_bytes=vmem_limit_bytes)


def _sigmoid(z):
    return 1.0 / (1.0 + jnp.exp(-z))


def _silu(z):
    return z * _sigmoid(z)


def _act(z, kind):
    return _sigmoid(z) if kind == "sigmoid" else _silu(z)


def _layer_norm(x, g, b, eps):
    mu = jnp.mean(x, axis=-1, keepdims=True)
    xc = x - mu
    var = jnp.mean(xc * xc, axis=-1, keepdims=True)
    return xc * lax.rsqrt(var + eps) * g + b


def _split_bf16(x, terms):
    parts = []
    rest = x
    for _ in range(terms):
        part = rest.astype(BF16).astype(F32)
        parts.append(part)
        rest = rest - part
    return parts


def _bdot(a, b, kind):
    dims = {"nn": NN_DIMS, "nt": NT_DIMS, "tn": TN_DIMS}[kind]
    return lax.dot_general(a.astype(BF16), b.astype(BF16), dims, preferred_element_type=F32)


def _mask_mm(mask, x):
    return sum(jnp.dot(mask, part.astype(BF16), preferred_element_type=F32) for part in _split_bf16(x, 2))


def _mm_mask(x, mask):
    return sum(jnp.dot(part.astype(BF16), mask, preferred_element_type=F32) for part in _split_bf16(x, 2))


def _chunk_tril_mask(ts):
    row = lax.broadcasted_iota(jnp.int32, (ts, ts), 0)
    col = lax.broadcasted_iota(jnp.int32, (ts, ts), 1)
    shift = CHUNK.bit_length() - 1
    same = lax.shift_right_logical(row, shift) == lax.shift_right_logical(col, shift)
    return jnp.logical_and(same, col <= row).astype(BF16)


def _segment_ones_mask(width, seg):
    row = lax.broadcasted_iota(jnp.int32, (width, width), 0)
    col = lax.broadcasted_iota(jnp.int32, (width, width), 1)
    shift = seg.bit_length() - 1
    return (lax.shift_right_logical(row, shift) == lax.shift_right_logical(col, shift)).astype(BF16)


def _mm_kernel(x_ref, w_ref, o_ref):
    o_ref[...] = jnp.dot(x_ref[...], w_ref[...], preferred_element_type=F32).astype(o_ref.dtype)


def _mm_vmem_limit(tm, k, tn, x_dtype, w_dtype, out_dtype):
    blocks = (tm * k * jnp.dtype(x_dtype).itemsize + k * tn * jnp.dtype(w_dtype).itemsize
              + tm * tn * jnp.dtype(out_dtype).itemsize)
    return max(V7X_VMEM_LIMIT_BYTES, (2 * blocks + tm * tn * 4) * 9 // 8)


def _matmul(x, w, *, out_dtype, tm, tn):
    m, k = x.shape
    n = w.shape[1]
    return pl.pallas_call(
        _mm_kernel,
        grid=(m // tm, n // tn),
        in_specs=[pl.BlockSpec((tm, k), lambda i, j: (i, 0)),
                  pl.BlockSpec((k, tn), lambda i, j: (0, j))],
        out_specs=pl.BlockSpec((tm, tn), lambda i, j: (i, j)),
        out_shape=jax.ShapeDtypeStruct((m, n), out_dtype),
        compiler_params=_params("parallel", "arbitrary",
                                vmem_limit_bytes=_mm_vmem_limit(tm, k, tn, x.dtype, w.dtype, out_dtype)),
        name="matmul",
    )(x, w)


def _gated_mm_body(x1_ref, w1_ref, b1_ref, x2_refs, w2_ref, o_ref, *, nb, act):
    a = x1_ref[...]
    acc = None
    for n in range(nb):
        z = jnp.dot(a, w1_ref[n], preferred_element_type=F32)
        if b1_ref is not None:
            z = z + b1_ref[n]
        lhs2 = a if x2_refs is None else x2_refs[n][...]
        lin = jnp.dot(lhs2, w2_ref[n], preferred_element_type=F32)
        t = _act(z, act) * lin
        acc = t if acc is None else acc + t
    o_ref[...] = acc.astype(o_ref.dtype)


def _gated_mm_kernel(*refs, nb, act, has_bias, n_x2):
    refs = list(refs)
    x1_ref, w1_ref = refs[0], refs[1]
    pos = 2
    b1_ref = None
    if has_bias:
        b1_ref = refs[pos]
        pos += 1
    x2_refs = refs[pos:pos + n_x2] if n_x2 else None
    pos += n_x2
    w2_ref, o_ref = refs[pos], refs[pos + 1]
    _gated_mm_body(x1_ref, w1_ref, b1_ref, x2_refs, w2_ref, o_ref, nb=nb, act=act)


def _gated_mm(x1, w1, w2, *, x2s=None, b1=None, act, out_dtype, tm, tn):
    m, k1 = x1.shape
    nb, _, n = w1.shape
    k2 = w2.shape[1]
    in_specs = [pl.BlockSpec((tm, k1), lambda i, j: (i, 0)),
                pl.BlockSpec((nb, k1, tn), lambda i, j: (0, 0, j))]
    args = [x1, w1]
    if b1 is not None:
        in_specs.append(pl.BlockSpec((nb, 1, tn), lambda i, j: (0, 0, j)))
        args.append(b1)
    n_x2 = 0
    if x2s is not None:
        n_x2 = len(x2s)
        for x2 in x2s:
            in_specs.append(pl.BlockSpec((tm, k2), lambda i, j: (i, 0)))
            args.append(x2)
    in_specs.append(pl.BlockSpec((nb, k2, tn), lambda i, j: (0, 0, j)))
    args.append(w2)
    vmem_bytes = (2 * (2 * (tm * k1 + nb * k1 * tn + n_x2 * tm * k2 + nb * k2 * tn)
                       + tm * tn * jnp.dtype(out_dtype).itemsize) + 3 * tm * tn * 4)
    return pl.pallas_call(
        functools.partial(_gated_mm_kernel, nb=nb, act=act, has_bias=b1 is not None, n_x2=n_x2),
        grid=(m // tm, n // tn),
        in_specs=in_specs,
        out_specs=pl.BlockSpec((tm, tn), lambda i, j: (i, j)),
        out_shape=jax.ShapeDtypeStruct((m, n), out_dtype),
        compiler_params=_params("parallel", "arbitrary",
                                vmem_limit_bytes=max(V7X_VMEM_LIMIT_BYTES, vmem_bytes * 9 // 8)),
        name="gated_matmul",
    )(*args)


def _grouped_gated_kernel(te_ref, nt_ref, x_ref, w1_ref, w2_ref, o_ref, w1_bf, w2_bf, *, act):
    i = pl.program_id(1)
    new_expert = jnp.logical_or(i == 0, te_ref[i] != te_ref[jnp.maximum(i - 1, 0)])

    @pl.when(new_expert)
    def _():
        w1_bf[...] = w1_ref[0].astype(BF16)
        w2_bf[...] = w2_ref[0].astype(BF16)

    @pl.when(i < nt_ref[0])
    def _():
        a = x_ref[...]
        z = jnp.dot(a, w1_bf[...], preferred_element_type=F32)
        lin = jnp.dot(a, w2_bf[...], preferred_element_type=F32)
        o_ref[...] = (_act(z, act) * lin).astype(o_ref.dtype)

    @pl.when(i >= nt_ref[0])
    def _():
        o_ref[...] = jnp.zeros_like(o_ref)


def _grouped_gated_mm(x, w1, w2, tile_expert, n_tiles, *, act, out_dtype, tm, tn):
    m, k = x.shape
    n = w1.shape[2]
    grid_spec = pltpu.PrefetchScalarGridSpec(
        num_scalar_prefetch=2,
        grid=(n // tn, m // tm),
        in_specs=[pl.BlockSpec((tm, k), lambda j, i, te, nt: (i, 0)),
                  pl.BlockSpec((1, k, tn), lambda j, i, te, nt: (te[i], 0, j)),
                  pl.BlockSpec((1, k, tn), lambda j, i, te, nt: (te[i], 0, j))],
        out_specs=pl.BlockSpec((tm, tn), lambda j, i, te, nt: (i, j)),
        scratch_shapes=[pltpu.VMEM((k, tn), BF16), pltpu.VMEM((k, tn), BF16)],
    )
    vmem_bytes = (2 * 2 * k * tn * 4 + 2 * k * tn * 2 + 2 * tm * k * 2
                  + 2 * tm * tn * jnp.dtype(out_dtype).itemsize + 2 * tm * tn * 4)
    return pl.pallas_call(
        functools.partial(_grouped_gated_kernel, act=act),
        grid_spec=grid_spec,
        out_shape=jax.ShapeDtypeStruct((m, n), out_dtype),
        compiler_params=_params("arbitrary", "arbitrary",
                                vmem_limit_bytes=max(V7X_VMEM_LIMIT_BYTES, vmem_bytes * 9 // 8)),
        name="grouped_gated_matmul",
    )(tile_expert, n_tiles, x, w1, w2)


def _grouped_mm_kernel(te_ref, nt_ref, x_ref, w_ref, o_ref):
    i = pl.program_id(0)

    @pl.when(i < nt_ref[0])
    def _():
        o_ref[...] = jnp.dot(x_ref[...], w_ref[0], preferred_element_type=F32)

    @pl.when(i >= nt_ref[0])
    def _():
        o_ref[...] = jnp.zeros_like(o_ref)


def _grouped_matmul(x, w, tile_expert, n_tiles, *, tm, tn):
    m, k = x.shape
    n = w.shape[2]
    grid_spec = pltpu.PrefetchScalarGridSpec(
        num_scalar_prefetch=2,
        grid=(m // tm, n // tn),
        in_specs=[pl.BlockSpec((tm, k), lambda i, j, te, nt: (i, 0)),
                  pl.BlockSpec((1, k, tn), lambda i, j, te, nt: (te[i], 0, j))],
        out_specs=pl.BlockSpec((tm, tn), lambda i, j, te, nt: (i, j)),
    )
    return pl.pallas_call(
        _grouped_mm_kernel,
        grid_spec=grid_spec,
        out_shape=jax.ShapeDtypeStruct((m, n), F32),
        compiler_params=_params("arbitrary", "arbitrary",
                                vmem_limit_bytes=_mm_vmem_limit(tm, k, tn, x.dtype, w.dtype, F32)),
        name="grouped_matmul",
    )(tile_expert, n_tiles, x, w)


def _proj_res_ln_kernel(h_ref, m_ref, w_ref, g_ref, b_ref, o_ref, obf_ref, *, alpha):
    y = alpha * h_ref[...] + jnp.dot(m_ref[...], w_ref[...], preferred_element_type=F32)
    out = _layer_norm(y, g_ref[...], b_ref[...], LN_EPS)
    o_ref[...] = out
    obf_ref[...] = out.astype(BF16)


def _proj_res_ln(h, m_in, w, g, b, *, alpha, tm):
    n, d = h.shape
    k = m_in.shape[1]
    vmem_bytes = 2 * (tm * d * 4 + tm * k * 2 + tm * d * 6 + k * d * 2) + 2 * tm * d * 4
    return pl.pallas_call(
        functools.partial(_proj_res_ln_kernel, alpha=alpha),
        grid=(n // tm,),
        in_specs=[pl.BlockSpec((tm, d), lambda i: (i, 0)),
                  pl.BlockSpec((tm, k), lambda i: (i, 0)),
                  pl.BlockSpec((k, d), lambda i: (0, 0)),
                  pl.BlockSpec((1, d), lambda i: (0, 0)),
                  pl.BlockSpec((1, d), lambda i: (0, 0))],
        out_specs=[pl.BlockSpec((tm, d), lambda i: (i, 0)),
                   pl.BlockSpec((tm, d), lambda i: (i, 0))],
        out_shape=[jax.ShapeDtypeStruct((n, d), F32), jax.ShapeDtypeStruct((n, d), BF16)],
        compiler_params=_params("parallel", vmem_limit_bytes=max(V7X_VMEM_LIMIT_BYTES, vmem_bytes * 9 // 8)),
        name="proj_res_ln",
    )(h, m_in, w, g, b)


def _add_ln_kernel(*refs, alpha, n_terms, weighted):
    h_ref, hbf_ref, p_ref = refs[0:3]
    a_refs = refs[3:3 + n_terms]
    pos = 3 + n_terms
    c_ref = None
    if weighted:
        c_ref = refs[pos]
        pos += 1
    wg_ref, bg_ref, wp_ref, g_ref, b_ref, o_ref, obf_ref = refs[pos:pos + 7]
    gate = _sigmoid(jnp.dot(hbf_ref[...], wg_ref[...], preferred_element_type=F32) + bg_ref[...])
    ple = gate * jnp.dot(p_ref[...], wp_ref[...], preferred_element_type=F32)
    y = alpha * h_ref[...] + ple
    for j in range(n_terms):
        t = a_refs[j][...]
        if weighted:
            t = t * c_ref[:, j:j + 1]
        y = y + t
    out = _layer_norm(y, g_ref[...], b_ref[...], LN_EPS)
    o_ref[...] = out
    obf_ref[...] = out.astype(BF16)


def _add_ln(h, h_bf, p_bf, terms, coef, ple_wg, ple_bg, ple_wp, g, b, *, alpha, tm):
    n, d = h.shape
    dp = p_bf.shape[1]
    row = pl.BlockSpec((tm, d), lambda i: (i, 0))
    vec = pl.BlockSpec((1, d), lambda i: (0, 0))
    in_specs = [row, row, pl.BlockSpec((tm, dp), lambda i: (i, 0))] + [row] * len(terms)
    args = [h, h_bf, p_bf] + list(terms)
    if coef is not None:
        in_specs.append(pl.BlockSpec((tm, coef.shape[1]), lambda i: (i, 0)))
        args.append(coef)
    in_specs += [pl.BlockSpec((d, d), lambda i: (0, 0)), vec, pl.BlockSpec((dp, d), lambda i: (0, 0)), vec, vec]
    args += [ple_wg, ple_bg, ple_wp, g, b]
    return pl.pallas_call(
        functools.partial(_add_ln_kernel, alpha=alpha, n_terms=len(terms), weighted=coef is not None),
        grid=(n // tm,),
        in_specs=in_specs,
        out_specs=[row, row],
        out_shape=[jax.ShapeDtypeStruct((n, d), F32), jax.ShapeDtypeStruct((n, d), BF16)],
        compiler_params=_params("parallel"),
        name="add_ln",
    )(*args)


def _conv_kernel(u_ref, dw_ref, dwb_ref, g_ref, b_ref, o_ref, buf, shifted, *, ts, rb, width):
    s = pl.program_id(1)

    @pl.when(s == 0)
    def _():
        buf[0:CONV_HALO, :] = jnp.zeros((CONV_HALO, buf.shape[1]), F32)

    buf[CONV_HALO:CONV_HALO + ts, :] = u_ref[0]
    lead = CONV_HALO - (width - 1)
    span = shifted.shape[1]
    for r in range(1, SUBLANES):
        shifted[r - 1] = buf[r:r + span, :]
    for r0 in range(0, ts, rb):
        acc = jnp.broadcast_to(dwb_ref[...], (rb, buf.shape[1]))
        for j in range(width):
            m, r = divmod(lead + j, SUBLANES)
            lo = r0 + SUBLANES * m
            rows = buf[lo:lo + rb, :] if r == 0 else shifted[r - 1, lo:lo + rb, :]
            acc = acc + rows * dw_ref[SUBLANES * j:SUBLANES * (j + 1), :]
        y = _silu(_layer_norm(acc, g_ref[...], b_ref[...], LN_EPS))
        o_ref[0, r0:r0 + rb, :] = y.astype(o_ref.dtype)
    buf[0:CONV_HALO, :] = buf[ts:ts + CONV_HALO, :]


def _conv_module(u, dw, dw_b, ln_g, ln_b, *, ts, rb):
    bsz, seq, c = u.shape
    width = dw.shape[0]
    assert rb == SUBLANES
    dw_rep = jnp.repeat(dw.astype(F32), SUBLANES, axis=0)
    vec = pl.BlockSpec((1, c), lambda b, s: (0, 0))
    return pl.pallas_call(
        functools.partial(_conv_kernel, ts=ts, rb=rb, width=width),
        grid=(bsz, seq // ts),
        in_specs=[pl.BlockSpec((1, ts, c), lambda b, s: (b, s, 0)),
                  pl.BlockSpec((width * SUBLANES, c), lambda b, s: (0, 0)),
                  vec, vec, vec],
        out_specs=pl.BlockSpec((1, ts, c), lambda b, s: (b, s, 0)),
        out_shape=jax.ShapeDtypeStruct((bsz, seq, c), BF16),
        scratch_shapes=[pltpu.VMEM((CONV_HALO + ts, c), F32),
                        pltpu.VMEM((SUBLANES - 1, CONV_HALO + ts - SUBLANES, c), F32)],
        compiler_params=_params("parallel", "arbitrary"),
        name="conv_module",
    )(u, dw_rep, dw_b.reshape(1, c), ln_g.reshape(1, c), ln_b.reshape(1, c))


def _hgrn_steps(zq, zf, zi, zg, lb, ng, tril, state_t, k_buf, b_buf, *, ts):
    nc = ts // CHUNK
    nsub = CHUNK // SUB
    q = _silu(zq) * (HGRN_EXPAND ** -0.5)
    k = (1.0 - lb) * _sigmoid(-zf)
    v = zi
    b = _mask_mm(tril, jnp.log(1.0 - k))
    b2 = b * LOG2_E
    k_buf[...] = jnp.log2(k) - b2
    b_buf[...] = b2

    row = lax.broadcasted_iota(jnp.int32, (CHUNK, CHUNK), 0)
    col = lax.broadcasted_iota(jnp.int32, (CHUNK, CHUNK), 1)
    row_sub = lax.shift_right_logical(row, SUB.bit_length() - 1)
    col_sub = lax.shift_right_logical(col, SUB.bit_length() - 1)
    earlier_sub = col_sub < row_sub
    same_sub_causal = jnp.logical_and(col_sub == row_sub, col <= row)
    strip_col = lax.broadcasted_iota(jnp.int32, (SUB, CHUNK), 1)
    yield

    o_intra, upd, decay, q_dec = [], [], [], []
    for c in range(nc):
        base = c * CHUNK
        q_c, k_c, b2_c, v_c = (t[base:base + CHUNK] for t in (q, k, b2, v))
        off_strips, diag_strips = [], []
        for i in range(nsub):
            lo = base + i * SUB
            q_i, b2_i = q[lo:lo + SUB], b2[lo:lo + SUB]
            if i == 0:
                off_strips.append(jnp.zeros((SUB, CHUNK), F32))
            else:
                beta = b_buf[lo - 1:lo, :]
                q_hat = q_i * jnp.exp2(b2_i - beta)
                k_hat = k_c * jnp.exp2(jnp.minimum(beta - b2_c, 0.0))
                off_strips.append(_bdot(q_hat, k_hat, "nt"))
            strip = jnp.zeros((SUB, CHUNK), F32)
            for j in range(SUB):
                sj = i * SUB + j
                e = jnp.exp2(b2_i + k_buf[lo + j:lo + j + 1, :])
                colv = jnp.sum(q_i * e, axis=-1, keepdims=True)
                strip = jnp.where(strip_col == sj, colv, strip)
            diag_strips.append(strip)
            yield
        att =(jnp.where(earlier_sub, jnp.concatenate(off_strips, axis=0), 0.0)
               + jnp.where(same_sub_causal, jnp.concatenate(diag_strips, axis=0), 0.0))
        o_intra.append(_bdot(att, v_c, "nn"))
        b_last = b_buf[base + CHUNK - 1:base + CHUNK, :]
        upd.append(_bdot(v_c, k_c * jnp.exp2(b_last - b2_c), "tn"))
        decay.append(jnp.exp2(b_last))
        q_dec.append(q_c * jnp.exp2(b2_c))
        yield

    st = state_t[...]
    outs = []
    for c in range(nc):
        outs.append(o_intra[c] + _bdot(q_dec[c], st, "nt"))
        st = st * decay[c] + upd[c]
    state_t[...] = st
    yield
    o = jnp.concatenate(outs, axis=0)
    o = o * lax.rsqrt(jnp.mean(o * o, axis=-1, keepdims=True) + LN_EPS) * ng
    return o * _silu(zg)


def _token_shift(z, prev_row, mu):
    rows = lax.broadcasted_iota(jnp.int32, z.shape, 0)
    shifted = jnp.where(rows == 0, prev_row, pltpu.roll(z, 1, axis=0))
    return z + (shifted - z) * mu


def _rwkv_prep_kernel(*refs, vres, c):
    (zrkv_ref, zl_ref, mu_rkv_ref, mu_l_ref, w0_ref, w2_ref, a0_ref, a2_ref, g2_ref,
     kk_ref, ka_ref) = refs[:11]
    pos = 11
    if vres:
        v0_ref, v2_ref, vfirst_ref = refs[pos:pos + 3]
        pos += 3
    r_o, k_o, v_o, ld_o, kk_o, a_o, g_o, prev_rkv, prev_l = refs[pos:pos + 9]
    s = pl.program_id(1)

    @pl.when(s == 0)
    def _():
        prev_rkv[...] = jnp.zeros_like(prev_rkv)
        prev_l[...] = jnp.zeros_like(prev_l)

    z = zrkv_ref[0].astype(F32)
    zl = zl_ref[0]
    ts = z.shape[0]
    z_mix = _token_shift(z, prev_rkv[...], mu_rkv_ref[...])
    zl_mix = _token_shift(zl, prev_l[...], mu_l_ref[...])
    prev_rkv[...] = z[ts - 1:ts, :]
    prev_l[...] = zl[ts - 1:ts, :]

    zr, zk, zv = z_mix[:, 0:c], z_mix[:, c:2 * c], z_mix[:, 2 * c:3 * c]
    zw, za, zg = zl_mix[:, 0:128], zl_mix[:, 128:256], zl_mix[:, 256:512]

    def lora(act_in, w_ref):
        return jnp.dot(act_in.astype(BF16), w_ref[...], preferred_element_type=F32)

    x = -(w0_ref[...] + lora(jnp.tanh(zw), w2_ref))
    softplus = jnp.maximum(x, 0.0) + jnp.log(1.0 + jnp.exp(-jnp.abs(x)))
    w = -softplus - 0.5
    a = _sigmoid(a0_ref[...] + lora(za, a2_ref))
    g = lora(_sigmoid(zg), g2_ref)
    v = zv
    if vres:
        zm = zl_mix[:, 512:640]
        v = zv + (vfirst_ref[0].astype(F32) - zv) * _sigmoid(v0_ref[...] + lora(zm, v2_ref))
    r_o[0] = zr.astype(r_o.dtype)
    k_o[0] = (zk * (1.0 + (a - 1.0) * ka_ref[...])).astype(k_o.dtype)
    v_o[0] = v.astype(v_o.dtype)
    ld_o[0] = -jnp.exp(w)
    kk_o[0] = (zk * kk_ref[...]).astype(kk_o.dtype)
    a_o[0] = a.astype(a_o.dtype)
    g_o[0] = g.astype(g_o.dtype)


def _rwkv_prep(z_rkv, z_lora, mu_rkv, mu_l, w0, w2, a0, a2, g2, k_k, k_a, v0, v2, v_first, *, ts):
    bsz, seq, c3 = z_rkv.shape
    c = c3 // 3
    lw = z_lora.shape[2]
    vres = v0 is not None

    def full(arr):
        return pl.BlockSpec(arr.shape, lambda b, s: (0,) * arr.ndim)

    tile = pl.BlockSpec((1, ts, c), lambda b, s: (b, s, 0))
    args = [z_rkv, z_lora, mu_rkv, mu_l, w0, w2, a0, a2, g2, k_k, k_a]
    in_specs = [pl.BlockSpec((1, ts, c3), lambda b, s: (b, s, 0)),
                pl.BlockSpec((1, ts, lw), lambda b, s: (b, s, 0))] + [full(a) for a in args[2:]]
    if vres:
        args += [v0, v2, v_first]
        in_specs += [full(v0), full(v2), tile]
    return pl.pallas_call(
        functools.partial(_rwkv_prep_kernel, vres=vres, c=c),
        grid=(bsz, seq // ts),
        in_specs=in_specs,
        out_specs=[tile] * 7,
        out_shape=[jax.ShapeDtypeStruct((bsz, seq, c), dt) for dt in (BF16, BF16, BF16, F32, BF16, BF16, BF16)],
        scratch_shapes=[pltpu.VMEM((1, c3), F32), pltpu.VMEM((1, lw), F32)],
        compiler_params=_params("parallel", "arbitrary"),
        name="rwkv_prep",
    )(*args)


def _rwkv_steps(r, k, v, ld, kk, a, g, rk, lng, lnb, tril, head_ones, state, *, ts, hg):
    n = RWKV_HEAD
    nc = ts // CHUNK
    width = hg * n
    ld = ld * LOG2_E
    lp = _mask_mm(tril, ld)
    norm = jnp.sqrt(_mm_mask(kk * kk, head_ones))
    kkn = kk / jnp.maximum(norm, 1e-12)
    bvec = kkn * a
    inv_p = jnp.exp2(-lp)
    a_t = -kkn * jnp.exp2(lp - ld)
    r_t = r * jnp.exp2(lp)
    b_t = bvec * inv_p
    k_t = k * inv_p

    row = lax.broadcasted_iota(jnp.int32, (CHUNK, CHUNK), 0)
    col = lax.broadcasted_iota(jnp.int32, (CHUNK, CHUNK), 1)
    incl = row >= col
    strict = row > col
    eye = (row == col).astype(F32)

    cells = [(c, hh) for c in range(nc) for hh in range(hg)]

    def blk(x, c, hh):
        return x[c * CHUNK:(c + 1) * CHUNK, hh * n:(hh + 1) * n]

    p_last = [jnp.exp2(lp[(c + 1) * CHUNK - 1:(c + 1) * CHUNK, :]) for c in range(nc)]
    to_end = jnp.concatenate(
        [jnp.exp2(lp[(c + 1) * CHUNK - 1:(c + 1) * CHUNK, :] - lp[c * CHUNK:(c + 1) * CHUNK, :]) for c in range(nc)],
        axis=0)
    b_h = bvec * to_end
    k_h = k * to_end
    yield

    def stage(fn):
        out = {}
        for idx, cell in enumerate(cells):
            out[cell] = fn(cell)
            if idx % CELLS_PER_YIELD == CELLS_PER_YIELD - 1:
                yield
        return out

    a_c = {cell: blk(a_t, *cell) for cell in cells}
    v_c = {cell: blk(v, *cell) for cell in cells}
    gram = yield from stage(lambda cell: _bdot(
        jnp.concatenate([a_c[cell], blk(r_t, *cell)], axis=0),
        jnp.concatenate([blk(b_t, *cell), blk(k_t, *cell)], axis=0), "nt"))
    t_ab = {cell: jnp.where(strict, gram[cell][0:CHUNK, 0:CHUNK], 0.0) for cell in cells}
    t_ak = {cell: jnp.where(strict, gram[cell][0:CHUNK, CHUNK:], 0.0) for cell in cells}
    t_rb = {cell: jnp.where(incl, gram[cell][CHUNK:, 0:CHUNK], 0.0) for cell in cells}
    t_rk = {cell: jnp.where(incl, gram[cell][CHUNK:, CHUNK:], 0.0) for cell in cells}

    power = yield from stage(lambda cell: _bdot(t_ab[cell], t_ab[cell], "nn"))
    inv = {cell: eye + t_ab[cell] for cell in cells}
    for _ in range(CHUNK.bit_length() - 3):
        both = yield from stage(lambda cell: _bdot(
            jnp.concatenate([power[cell], inv[cell]], axis=0), power[cell], "nn"))
        power = {cell: both[cell][0:CHUNK] for cell in cells}
        inv = {cell: inv[cell] + both[cell][CHUNK:] for cell in cells}
    last = yield from stage(lambda cell: _bdot(inv[cell], power[cell], "nn"))
    inv = {cell: inv[cell] + last[cell] for cell in cells}

    tv = yield from stage(lambda cell: _bdot(
        jnp.concatenate([t_ak[cell], t_rk[cell]], axis=0), v_c[cell], "nn"))
    wu = yield from stage(lambda cell: _bdot(
        inv[cell], jnp.concatenate([a_c[cell], tv[cell][0:CHUNK]], axis=1), "nn"))
    qy = yield from stage(lambda cell: _bdot(t_rb[cell], wu[cell], "nn"))
    gh = yield from stage(lambda cell: _bdot(blk(b_h, *cell), wu[cell], "tn"))
    kv = yield from stage(lambda cell: _bdot(blk(k_h, *cell), v_c[cell], "tn"))
    q_g, y0, h_t = {}, {}, {}
    for cell in cells:
        c, hh = cell
        q_eff = blk(r_t, *cell) + qy[cell][:, 0:n]
        g_t = eye * p_last[c][:, hh * n:(hh + 1) * n] + gh[cell][:, 0:n]
        q_g[cell] = jnp.concatenate([q_eff, g_t], axis=0)
        y0[cell] = qy[cell][:, n:] + tv[cell][CHUNK:]
        h_t[cell] = gh[cell][:, n:] + kv[cell]
    yield

    z = [state[hh] for hh in range(hg)]
    y_rows = []
    for c in range(nc):
        y_heads = []
        for hh in range(hg):
            res = _bdot(q_g[(c, hh)], z[hh], "nn")
            y_heads.append(res[0:CHUNK] + y0[(c, hh)])
            z[hh] = res[CHUNK:] + h_t[(c, hh)]
        y_rows.append(jnp.concatenate(y_heads, axis=1))
        yield
    for hh in range(hg):
        state[hh] = z[hh]
    y = jnp.concatenate(y_rows, axis=0)

    inv_n = 1.0 / n
    mu = _mm_mask(y, head_ones) * inv_n
    yc = y - mu
    var = _mm_mask(yc * yc, head_ones) * inv_n
    yn = yc * lax.rsqrt(var + RWKV_LN_EPS) * lng + lnb
    bonus = _mm_mask(r * k * rk, head_ones) * v
    return (yn + bonus) * g


def _round_robin(gens):
    results = [None] * len(gens)
    live = list(range(len(gens)))
    while live:
        for idx in list(live):
            try:
                next(gens[idx])
            except StopIteration as stop:
                results[idx] = stop.value
                live.remove(idx)
    return results


def _dual_scan_kernel(r_ref, k_ref, v_ref, ld_ref, kk_ref, a_ref, g_ref, rk_ref, lng_ref, lnb_ref,
                      zq_ref, zf_ref, zi_ref, zg_ref, lb_ref, ng_ref, tril_ref, head_ones_ref,
                      o_rwkv_ref, o_hgrn_ref, r_state, h_state, k_buf, b_buf, *, ts, hg, hh):
    s = pl.program_id(2)

    @pl.when(s == 0)
    def _():
        r_state[...] = jnp.zeros_like(r_state)
        h_state[...] = jnp.zeros_like(h_state)

    dh = HGRN_EXPAND
    r, k, v, ld, kk, a, g = (ref[0].astype(F32) for ref in (r_ref, k_ref, v_ref, ld_ref, kk_ref, a_ref, g_ref))
    tril = tril_ref[...]
    gens = [_rwkv_steps(r, k, v, ld, kk, a, g, rk_ref[...], lng_ref[...], lnb_ref[...], tril, head_ones_ref[...],
                        r_state, ts=ts, hg=hg)]
    for j in range(hh):
        cs = slice(j * dh, (j + 1) * dh)
        gens.append(_hgrn_steps(zq_ref[0, :, cs], zf_ref[0, :, cs], zi_ref[0, :, cs], zg_ref[0, :, cs],
                                lb_ref[:, cs], ng_ref[...], tril, h_state.at[j], k_buf.at[j], b_buf.at[j], ts=ts))
    outs = _round_robin(gens)
    o_rwkv_ref[0] = outs[0].astype(o_rwkv_ref.dtype)
    o_hgrn_ref[0] = jnp.concatenate(outs[1:], axis=1).astype(o_hgrn_ref.dtype)


def _dual_scan(r, k, v, ld, kk, a, g, r_k, lnx_g, lnx_b, z_hgrn, lb, norm_g, *, ts, groups):
    bsz, seq, c = r.shape
    wr = c // groups
    hg = wr // RWKV_HEAD
    dh = HGRN_EXPAND
    ch = z_hgrn.shape[2] // 4
    wh = ch // groups
    hh = wh // dh
    rtile = pl.BlockSpec((1, ts, wr), lambda b, h, s: (b, s, h))
    rvec = pl.BlockSpec((1, wr), lambda b, h, s: (0, h))
    htile = pl.BlockSpec((1, ts, wh), lambda b, h, s: (b, s, h))

    def zspec(group):
        return pl.BlockSpec((1, ts, wh), lambda b, h, s: (b, s, group * groups + h))

    return pl.pallas_call(
        functools.partial(_dual_scan_kernel, ts=ts, hg=hg, hh=hh),
        grid=(bsz, groups, seq // ts),
        in_specs=[rtile] * 7 + [rvec] * 3 + [zspec(0), zspec(1), zspec(2), zspec(3),
                                             pl.BlockSpec((1, wh), lambda b, h, s: (0, h)),
                                             pl.BlockSpec((1, dh), lambda b, h, s: (0, 0)),
                                             pl.BlockSpec((ts, ts), lambda b, h, s: (0, 0)),
                                             pl.BlockSpec((wr, wr), lambda b, h, s: (0, 0))],
        out_specs=[rtile, htile],
        out_shape=[jax.ShapeDtypeStruct((bsz, seq, c), BF16), jax.ShapeDtypeStruct((bsz, seq, ch), BF16)],
        scratch_shapes=[pltpu.VMEM((hg, RWKV_HEAD, RWKV_HEAD), F32),
                        pltpu.VMEM((hh, dh, dh), F32),
                        pltpu.VMEM((hh, ts, dh), F32),
                        pltpu.VMEM((hh, ts, dh), F32)],
        compiler_params=_params("parallel", "parallel", "arbitrary"),
        name="dual_scan",
    )(r, k, v, ld, kk, a, g, r_k.reshape(1, c), lnx_g.reshape(1, c), lnx_b.reshape(1, c),
      z_hgrn, z_hgrn, z_hgrn, z_hgrn, lb.reshape(1, ch), norm_g.reshape(1, dh),
      _chunk_tril_mask(ts), _segment_ones_mask(wr, RWKV_HEAD))


def _router_kernel(h_ref, w_ref, idx_ref, wt_ref, *, n_experts):
    h_hi, h_lo = (t.astype(BF16) for t in _split_bf16(h_ref[...], 2))
    w_hi, w_lo = (t.astype(BF16) for t in _split_bf16(w_ref[...], 2))
    logits = (jnp.dot(h_hi, w_hi, preferred_element_type=F32) + jnp.dot(h_lo, w_hi, preferred_element_type=F32)
              + jnp.dot(h_hi, w_lo, preferred_element_type=F32))
    lane = lax.broadcasted_iota(jnp.int32, logits.shape, 1)
    neg = jnp.float32(-jnp.inf)
    big = jnp.int32(logits.shape[1])
    x = jnp.where(lane < n_experts, logits, neg)
    m1 = jnp.max(x, axis=-1, keepdims=True)
    i1 = jnp.min(jnp.where(x == m1, lane, big), axis=-1, keepdims=True)
    x2 = jnp.where(lane == i1, neg, x)
    m2 = jnp.max(x2, axis=-1, keepdims=True)
    i2 = jnp.min(jnp.where(x2 == m2, lane, big), axis=-1, keepdims=True)
    e2 = jnp.exp(m2 - m1)
    w1 = 1.0 / (1.0 + e2)
    w2 = e2 / (1.0 + e2)
    idx_ref[...] = jnp.where(lane == 0, i1, jnp.where(lane == 1, i2, 0))
    wt_ref[...] = jnp.where(lane == 0, w1, jnp.where(lane == 1, w2, 0.0))


def _router(h, w_router, *, tm):
    n, d = h.shape
    n_experts = w_router.shape[1]
    w_pad = jnp.zeros((d, 128), F32).at[:, :n_experts].set(w_router)
    out = pl.BlockSpec((tm, 128), lambda i: (i, 0))
    return pl.pallas_call(
        functools.partial(_router_kernel, n_experts=n_experts),
        grid=(n // tm,),
        in_specs=[pl.BlockSpec((tm, d), lambda i: (i, 0)),
                  pl.BlockSpec((d, 128), lambda i: (0, 0))],
        out_specs=[out, out],
        out_shape=[jax.ShapeDtypeStruct((n, 128), jnp.int32), jax.ShapeDtypeStruct((n, 128), F32)],
        compiler_params=_params("parallel"),
        name="moe_router",
    )(h, w_pad)


def _moe(h, h_bf, w_router, wg, wu, wd, *, tm):
    n, d = h.shape
    n_experts = wg.shape[0]
    idx_pad, wt_pad = _router(h, w_router, tm=512)
    e_flat = idx_pad[:, :TOP_K].reshape(-1)
    onehot = (e_flat[:, None] == jnp.arange(n_experts, dtype=jnp.int32)[None, :]).astype(jnp.int32)
    csum = jnp.cumsum(onehot, axis=0)
    rank = jnp.sum(csum * onehot, axis=1) - 1
    counts = csum[-1]
    padded = ((counts + tm - 1) // tm) * tm
    ends = jnp.cumsum(padded)
    starts = ends - padded
    pos = starts[e_flat] + rank
    n_rows = TOP_K * n + n_experts * tm
    src = jnp.zeros((n_rows,), jnp.int32).at[pos].set(jnp.arange(TOP_K * n, dtype=jnp.int32) // TOP_K)
    n_tiles = (ends[-1] // tm).astype(jnp.int32).reshape(1)
    tile_start = jnp.arange(n_rows // tm, dtype=jnp.int32) * tm
    tile_expert = jnp.minimum(jnp.searchsorted(ends, tile_start, side="right"), n_experts - 1).astype(jnp.int32)
    x_sorted = h_bf.at[src].get(mode="promise_in_bounds")
    h1 = _grouped_gated_mm(x_sorted, wg, wu, tile_expert, n_tiles, act="silu", out_dtype=BF16, tm=tm, tn=1024)
    y_sorted = _grouped_matmul(h1, wd, tile_expert, n_tiles, tm=tm, tn=1024)
    pos2 = pos.reshape(n, TOP_K)
    y0 = y_sorted.at[pos2[:, 0]].get(mode="promise_in_bounds")
    y1 = y_sorted.at[pos2[:, 1]].get(mode="promise_in_bounds")
    return [y0, y1], wt_pad


def _pad_cols(w, width):
    return jnp.pad(w, ((0, 0), (0, width - w.shape[1])))


def _pad_rows(w, height):
    return jnp.pad(w, ((0, height - w.shape[0]), (0, 0)))


def kernel(x, p, w_in, w_in_vres, conv_dw, conv_dw_b, conv_ln_g, conv_ln_b, rwkv_mu, rwkv_mu_vres, rwkv_w0, rwkv_w2, rwkv_a0, rwkv_a2, rwkv_v0, rwkv_v2, rwkv_g2, rwkv_kk, rwkv_ka, rwkv_rk, rwkv_lnx_g, rwkv_lnx_b, hgrn_lb, hgrn_norm_g, w_branch, b_gate, w_out, ln1_g, ln1_b, ffn_wg, ffn_wu, ffn_wd, moe_router, moe_wg, moe_wu, moe_wd, ple_wp, ple_wg, ple_bg, ln2_g, ln2_b):
    bsz, seq, d = x.shape
    depth = w_in.shape[0]
    n = bsz * seq
    cc = conv_dw.shape[2]
    cr = rwkv_w0.shape[1]
    ch = hgrn_lb.shape[1]
    d_decay, d_aaa, d_gate, d_mv = rwkv_w2.shape[1], rwkv_a2.shape[1], rwkv_g2.shape[1], rwkv_v2.shape[1]
    n_branch = w_branch.shape[1]
    alpha = float((2 * depth) ** 0.25)
    tm = min(1024, n)
    ts = min(256, seq)

    lb_soft = jax.nn.softmax(hgrn_lb.astype(F32), axis=0)
    lower_bounds = jnp.cumsum(lb_soft, axis=0) - lb_soft[0]

    h = x.reshape(n, d)
    h_bf = h.astype(BF16)
    v_first = None
    for i in range(depth):
        vres = i > 0
        wi = w_in[i]
        o_rwkv = 2 * cc
        o_lora = o_rwkv + 3 * cr
        o_hgrn = o_lora + d_decay + d_aaa + d_gate
        o_gate = o_hgrn + 4 * ch

        w_val = wi[:, 0:cc].astype(BF16)[None]
        w_glu = wi[:, cc:2 * cc].astype(BF16)[None]
        u = _gated_mm(h_bf, w_glu, w_val, act="sigmoid", out_dtype=F32, tm=tm, tn=min(1024, cc))
        y_conv = _conv_module(u.reshape(bsz, seq, cc), conv_dw[i], conv_dw_b[i], conv_ln_g[i], conv_ln_b[i],
                              ts=ts, rb=SUBLANES).reshape(n, cc)

        z_rkv = _matmul(h_bf, wi[:, o_rwkv:o_lora].astype(BF16), out_dtype=BF16, tm=tm, tn=min(1024, cr))
        lora_cols = [_pad_cols(wi[:, o_lora:o_lora + d_decay], 128),
                     _pad_cols(wi[:, o_lora + d_decay:o_lora + d_decay + d_aaa], 128),
                     _pad_cols(wi[:, o_lora + d_decay + d_aaa:o_hgrn], 256)]
        mu_i = rwkv_mu[i]
        mu_cols = [_pad_cols(mu_i[None, 3 * cr:3 * cr + d_decay], 128),
                   _pad_cols(mu_i[None, 3 * cr + d_decay:3 * cr + d_decay + d_aaa], 128),
                   _pad_cols(mu_i[None, 3 * cr + d_decay + d_aaa:], 256)]
        if vres:
            lora_cols.append(_pad_cols(w_in_vres[i - 1], 128))
            mu_cols.append(_pad_cols(rwkv_mu_vres[i - 1][None], 128))
        w_lora = jnp.concatenate(lora_cols, axis=1).astype(BF16)
        mu_l = jnp.concatenate(mu_cols, axis=1)
        z_lora = _matmul(h_bf, w_lora, out_dtype=F32, tm=tm, tn=w_lora.shape[1])
        r, k, v, ld, kk, a, g = _rwkv_prep(
            z_rkv.reshape(bsz, seq, 3 * cr), z_lora.reshape(bsz, seq, -1),
            mu_i[None, :3 * cr], mu_l,
            rwkv_w0[i][None], _pad_rows(rwkv_w2[i], 128).astype(BF16),
            rwkv_a0[i][None], _pad_rows(rwkv_a2[i], 128).astype(BF16),
            _pad_rows(rwkv_g2[i], 256).astype(BF16),
            rwkv_kk[i][None], rwkv_ka[i][None],
            rwkv_v0[i - 1][None] if vres else None,
            _pad_rows(rwkv_v2[i - 1], 128).astype(BF16) if vres else None,
            v_first, ts=ts)
        if not vres:
            v_first = v
        z_hgrn = _matmul(h_bf, wi[:, o_hgrn:o_gate].astype(BF16), out_dtype=F32, tm=tm, tn=min(1024, 4 * ch))
        y_rwkv, y_hgrn = _dual_scan(r, k, v, ld, kk, a, g, rwkv_rk[i], rwkv_lnx_g[i], rwkv_lnx_b[i],
                                    z_hgrn.reshape(bsz, seq, 4 * ch), lower_bounds[i], hgrn_norm_g[i],
                                    ts=ts, groups=SCAN_GROUPS)
        y_rwkv = y_rwkv.reshape(n, cr)
        y_hgrn = y_hgrn.reshape(n, ch)

        w_g = wi[:, o_gate:o_gate + n_branch * d].reshape(d, n_branch, d).transpose(1, 0, 2).astype(BF16)
        merged = _gated_mm(h_bf, w_g, w_branch[i].astype(BF16), x2s=[y_conv, y_rwkv, y_hgrn],
                           b1=b_gate[i][:, None, :], act="sigmoid", out_dtype=BF16, tm=tm, tn=min(512, d))
        h, h_bf = _proj_res_ln(h, merged, w_out[i].astype(BF16), ln1_g[i][None], ln1_b[i][None],
                               alpha=alpha, tm=min(512, n))

        j = i // 2
        if i % 2 == 0:
            h1 = _gated_mm(h_bf, ffn_wg[j].astype(BF16)[None], ffn_wu[j].astype(BF16)[None],
                           act="silu", out_dtype=BF16, tm=tm, tn=1024)
            terms, coef = [_matmul(h1, ffn_wd[j].astype(BF16), out_dtype=F32, tm=min(512, n), tn=1024)], None
        else:
            terms, coef = _moe(h, h_bf, moe_router[j], moe_wg[j], moe_wu[j],
                               moe_wd[j].astype(BF16), tm=min(512, n))
        h, h_bf = _add_ln(h, h_bf, p[i].reshape(n, -1).astype(BF16), terms, coef,
                          ple_wg[i].astype(BF16), ple_bg[i][None], ple_wp[i].astype(BF16),
                          ln2_g[i][None], ln2_b[i][None], alpha=alpha, tm=min(256, n))
    return h.reshape(bsz, seq, d)
```
